```python
import math
import jax, jax.numpy as jnp
from jax import lax
import numpy as np

D_MODEL = 1024
BATCH = 2
SEQ = 8192
DEPTH = 1

HEAD_DIM = 64
N_HEADS_A = 8
N_KV_A = 2
GQA_GROUP = N_HEADS_A // N_KV_A
N_HEADS_B = 8
WIDTH_A = N_HEADS_A * HEAD_DIM
WIDTH_B = N_HEADS_B * HEAD_DIM
KV_WIDTH_A = N_KV_A * HEAD_DIM
IN_WIDTH = WIDTH_A + 2 * KV_WIDTH_A + 3 * WIDTH_B
D_FF = 2816
CONV_WIDTH = 3
GRID_W = 64
ROPE_THETA = 10000.0
ROPE_AXIS_DIM = HEAD_DIM // 2
Q_BLOCK = 128
DIL_BLOCK = 64
DILATED_PATTERNS = ((128, 1), (512, 4), (2048, 16))
NORM_EPS = 1e-6
NEG_INF = -1e30

kernel_name = "hybrid_gqa_axialrope_dilated_alibi_convffn"


def rms_norm(x, g):
    xf = x.astype(jnp.float32)
    y = xf * lax.rsqrt(jnp.mean(xf * xf, axis=-1, keepdims=True) + NORM_EPS)
    return (y * g.astype(jnp.float32)).astype(x.dtype)


def axial_rope_tables(seq_len):
    rows = seq_len // GRID_W
    row = jnp.repeat(jnp.arange(rows, dtype=jnp.float32), GRID_W)
    col = jnp.tile(jnp.arange(GRID_W, dtype=jnp.float32), rows)
    inv = ROPE_THETA ** (-jnp.arange(0, ROPE_AXIS_DIM, 2, dtype=jnp.float32) / ROPE_AXIS_DIM)
    ang_r = row[:, None] * inv[None, :]
    ang_c = col[:, None] * inv[None, :]
    return jnp.cos(ang_r), jnp.sin(ang_r), jnp.cos(ang_c), jnp.sin(ang_c)


def _rotate(u, c, s):
    half = u.shape[-1] // 2
    u1, u2 = u[..., :half], u[..., half:]
    c = c[:, None, :]
    s = s[:, None, :]
    return jnp.concatenate([u1 * c - u2 * s, u2 * c + u1 * s], axis=-1)


def apply_axial_rope(x, tables):
    cos_r, sin_r, cos_c, sin_c = tables
    xf = x.astype(jnp.float32)
    out = jnp.concatenate([_rotate(xf[..., :ROPE_AXIS_DIM], cos_r, sin_r),
                           _rotate(xf[..., ROPE_AXIS_DIM:], cos_c, sin_c)], axis=-1)
    return out.astype(x.dtype)


def global_gqa_attention(q, k, v):
    b, s = q.shape[0], q.shape[1]
    nb = s // Q_BLOCK
    qg = q.reshape(b, nb, Q_BLOCK, N_KV_A, GQA_GROUP, HEAD_DIM).transpose(1, 0, 2, 3, 4, 5)
    scale = HEAD_DIM ** -0.5

    def block(qb):
        sc = jnp.einsum('bqkgd,bskd->bkgqs', qb, k, preferred_element_type=jnp.float32) * scale
        p = jax.nn.softmax(sc, axis=-1)
        return jnp.einsum('bkgqs,bskd->bqkgd', p.astype(v.dtype), v)

    o = lax.map(block, qg)
    return o.transpose(1, 0, 2, 3, 4, 5).reshape(b, s, WIDTH_A)


def alibi_slopes(n_heads):
    return 2.0 ** (-8.0 * jnp.arange(1, n_heads + 1, dtype=jnp.float32) / n_heads)


def dilated_window_attention(q, k, v, window, dilation, slopes):
    b, s, h, hd = q.shape
    d = dilation
    n_side = (window // 2) // d
    L = s // d
    nb = -(-L // DIL_BLOCK)
    Lp = nb * DIL_BLOCK

    def strided(a):
        return a.reshape(b, L, d, h, hd).transpose(0, 2, 1, 3, 4)

    qs = jnp.pad(strided(q), ((0, 0), (0, 0), (0, Lp - L), (0, 0), (0, 0)))
    qb = qs.reshape(b, d, nb, DIL_BLOCK, h, hd)

    def banded(a):
        ap = jnp.pad(strided(a), ((0, 0), (0, 0), (DIL_BLOCK, Lp - L + DIL_BLOCK), (0, 0), (0, 0)))
        ap = ap.reshape(b, d, nb + 2, DIL_BLOCK, h, hd)
        return jnp.concatenate([ap[:, :, :-2], ap[:, :, 1:-1], ap[:, :, 2:]], axis=3)

    kw = banded(k)
    vw = banded(v)

    qi = jnp.arange(DIL_BLOCK)[:, None]
    kj = jnp.arange(3 * DIL_BLOCK)[None, :]
    off = kj - DIL_BLOCK - qi
    key_idx = jnp.arange(nb)[:, None, None] * DIL_BLOCK - DIL_BLOCK + kj[None]
    valid = (jnp.abs(off) <= n_side)[None] & (key_idx >= 0) & (key_idx < L)
    dist = (jnp.abs(off) * d).astype(jnp.float32)
    bias = jnp.where(valid[:, None], -slopes[None, :, None, None] * dist[None, None],
                     NEG_INF)

    sc = jnp.einsum('brnqhd,brnkhd->brnhqk', qb, kw, preferred_element_type=jnp.float32)
    sc = sc * (hd ** -0.5) + bias[None, None]
    lse = jax.nn.logsumexp(sc, axis=-1)
    p = jnp.exp(sc - lse[..., None])
    o = jnp.einsum('brnhqk,brnkhd->brnqhd', p.astype(v.dtype), vw)

    o = o.reshape(b, d, Lp, h, hd)[:, :, :L].transpose(0, 2, 1, 3, 4).reshape(b, s, h, hd)
    lse = lse.transpose(0, 1, 2, 4, 3).reshape(b, d, Lp, h)[:, :, :L]
    lse = lse.transpose(0, 2, 1, 3).reshape(b, s, h)
    return o, lse


def dilated_mixture_attention(q, k, v):
    slopes = alibi_slopes(N_HEADS_B)
    outs, lses = [], []
    for window, dilation in DILATED_PATTERNS:
        o, lse = dilated_window_attention(q, k, v, window, dilation, slopes)
        outs.append(o)
        lses.append(lse)
    w = jax.nn.softmax(jnp.stack(lses, axis=0), axis=0)
    o = jnp.sum(w[..., None].astype(q.dtype) * jnp.stack(outs, axis=0), axis=0)
    return o.reshape(q.shape[0], q.shape[1], WIDTH_B)


def conv_ffn(h, w_up, conv_w, conv_b, w_down):
    u = h @ w_up
    c = u.shape[-1]
    u = lax.conv_general_dilated(u, conv_w[:, None, :].astype(u.dtype), window_strides=(1,),
                                 padding=((CONV_WIDTH // 2, CONV_WIDTH // 2),),
                                 dimension_numbers=('NWC', 'WIO', 'NWC'),
                                 feature_group_count=c) + conv_b
    gate, val = u[..., :D_FF], u[..., D_FF:]
    return (jax.nn.gelu(gate) * val) @ w_down


def setup_inputs(seed: int = 0) -> dict:
    key = jax.random.key(seed)
    ks = jax.random.split(key, 16)
    f32 = jnp.float32

    def gain(k, n):
        return 1.0 + 0.05 * jax.random.normal(k, (n,), f32)

    return {
        "x": jax.random.normal(ks[0], (BATCH, SEQ, D_MODEL), f32),
        "norm1_g": gain(ks[1], D_MODEL),
        "w_in": jax.random.normal(ks[2], (D_MODEL, IN_WIDTH), f32) * D_MODEL ** -0.5,
        "qa_norm_g": gain(ks[3], HEAD_DIM),
        "ka_norm_g": gain(ks[4], HEAD_DIM),
        "qb_norm_g": gain(ks[5], HEAD_DIM),
        "kb_norm_g": gain(ks[6], HEAD_DIM),
        "outa_norm_g": gain(ks[7], WIDTH_A),
        "outb_norm_g": gain(ks[8], WIDTH_B),
        "w_out": jax.random.normal(ks[9], (WIDTH_A + WIDTH_B, D_MODEL), f32) * (WIDTH_A + WIDTH_B) ** -0.5,
        "norm2_g": gain(ks[10], D_MODEL),
        "w_up": jax.random.normal(ks[11], (D_MODEL, 2 * D_FF), f32) * D_MODEL ** -0.5,
        "conv_w": jax.random.normal(ks[12], (CONV_WIDTH, 2 * D_FF), f32) * CONV_WIDTH ** -0.5,
        "conv_b": 0.02 * jax.random.normal(ks[13], (2 * D_FF,), f32),
        "w_down": jax.random.normal(ks[14], (D_FF, D_MODEL), f32) * D_FF ** -0.5,
    }


def reference(x, norm1_g, w_in, qa_norm_g, ka_norm_g, qb_norm_g, kb_norm_g,
              outa_norm_g, outb_norm_g, w_out, norm2_g, w_up, conv_w, conv_b, w_down):
    b, s, _ = x.shape
    tables = axial_rope_tables(s)
    for _layer in range(DEPTH):
        hn = rms_norm(x, norm1_g)
        proj = hn @ w_in
        o1 = WIDTH_A
        o2 = o1 + KV_WIDTH_A
        o3 = o2 + KV_WIDTH_A
        o4 = o3 + WIDTH_B
        o5 = o4 + WIDTH_B
        qa = proj[..., :o1].reshape(b, s, N_HEADS_A, HEAD_DIM)
        ka = proj[..., o1:o2].reshape(b, s, N_KV_A, HEAD_DIM)
        va = proj[..., o2:o3].reshape(b, s, N_KV_A, HEAD_DIM)
        qb = proj[..., o3:o4].reshape(b, s, N_HEADS_B, HEAD_DIM)
        kb = proj[..., o4:o5].reshape(b, s, N_HEADS_B, HEAD_DIM)
        vb = proj[..., o5:].reshape(b, s, N_HEADS_B, HEAD_DIM)

        qa = apply_axial_rope(rms_norm(qa, qa_norm_g), tables)
        ka = apply_axial_rope(rms_norm(ka, ka_norm_g), tables)
        out_a = global_gqa_attention(qa, ka, va)

        out_b = dilated_mixture_attention(rms_norm(qb, qb_norm_g), rms_norm(kb, kb_norm_g), vb)

        mixed = jnp.concatenate([rms_norm(out_a, outa_norm_g), rms_norm(out_b, outb_norm_g)], axis=-1)
        x = x + mixed @ w_out

        x = x + conv_ffn(rms_norm(x, norm2_g), w_up, conv_w, conv_b, w_down)
    return x
```

```python
import functools
import math

import jax
import jax.numpy as jnp
from jax import lax
from jax.experimental import pallas as pl
from jax.experimental.pallas import tpu as pltpu

F32 = jnp.float32
BF16 = jnp.bfloat16

HEAD_DIM = 64
N_HEADS_A = 8
N_KV_A = 2
GQA_GROUP = N_HEADS_A // N_KV_A
N_HEADS_B = 8
WIDTH_A = N_HEADS_A * HEAD_DIM
WIDTH_B = N_HEADS_B * HEAD_DIM
KV_WIDTH_A = N_KV_A * HEAD_DIM
GRID_W = 64
ROPE_THETA = 10000.0
ROPE_AXIS_DIM = HEAD_DIM // 2
DIL_BLOCK = 64
DILATED_PATTERNS = ((128, 1), (512, 4), (2048, 16))
NORM_EPS = 1e-6
NEG_INF = -1e30

LANES = 128
MXU_DIM = 256
VMEM_LIMIT_BYTES = 56 * 1024 * 1024

PROJ_TM = 512
ATT_TQ = 512
ATT_TK = 512
DIL_TL = 512
DIL_SUB = 128
MERGE_TM = 512
FFN_TM = 512
FFN_FC = 256
FFN_HALO = 16
LSE_LANES = 16


def _nt_dot(a, b):
    return lax.dot_general(a, b, (((1,), (1,)), ((), ())), preferred_element_type=F32)


def _proj_kernel(x_ref, g1_ref, wa_ref, wb_ref, ga_ref, cos_ref, sin_ref, gqb_ref, gkb_ref,
                 qt_ref, k_ref, vt_ref, qb_ref, kb_ref, vb_ref):
    tm = x_ref.shape[1]
    x = x_ref[0]
    ms = jnp.mean(x * x, axis=-1, keepdims=True)
    hn = (x * lax.rsqrt(ms + NORM_EPS) * g1_ref[...]).astype(BF16)
    ya = jnp.dot(hn, wa_ref[...], preferred_element_type=F32)
    yb = jnp.dot(hn, wb_ref[...], preferred_element_type=F32)

    ya_t = ya.T
    vt_ref[0] = ya_t[WIDTH_A + KV_WIDTH_A:].astype(BF16)
    cos_t = cos_ref[...]
    sin_t = sin_ref[...]
    reps = tm // LANES
    zeros = jnp.zeros((HEAD_DIM, tm), BF16)
    k_parts = []
    q16 = ROPE_AXIS_DIM // 2
    for h in range(N_HEADS_A + N_KV_A):
        u = ya_t[HEAD_DIM * h:HEAD_DIM * (h + 1)]
        r = lax.rsqrt(jnp.mean(u * u, axis=0, keepdims=True) + NORM_EPS)
        g = jnp.tile(ga_ref[HEAD_DIM * h:HEAD_DIM * (h + 1), :], (1, reps))
        un = u * r * g
        partner = jnp.concatenate(
            [un[q16:2 * q16], un[0:q16], un[3 * q16:4 * q16], un[2 * q16:3 * q16]], axis=0)
        o = un * cos_t + partner * sin_t
        if h < N_HEADS_A:
            ob = o.astype(BF16)
            pieces = [ob, zeros] if h // GQA_GROUP == 0 else [zeros, ob]
            qt_ref[0, h] = jnp.concatenate(pieces, axis=0)
        else:
            k_parts.append(o)
    k_ref[0] = jnp.concatenate(k_parts, axis=0).T.astype(BF16)

    ri = lax.broadcasted_iota(jnp.int32, (MXU_DIM, MXU_DIM), 0) // HEAD_DIM
    ci = lax.broadcasted_iota(jnp.int32, (MXU_DIM, MXU_DIM), 1) // HEAD_DIM
    ones_bd = jnp.where(ri == ci, 1.0, 0.0).astype(BF16)

    def head_norm(y, g_ref):
        sq = (y * y).astype(BF16)
        ss = jnp.concatenate(
            [jnp.dot(sq[:, c:c + MXU_DIM], ones_bd, preferred_element_type=F32)
             for c in range(0, y.shape[1], MXU_DIM)], axis=1)
        return (y * lax.rsqrt(ss * (1.0 / HEAD_DIM) + NORM_EPS) * g_ref[...]).astype(BF16)

    qb_ref[0] = head_norm(yb[:, :WIDTH_B], gqb_ref)
    kb_ref[0] = head_norm(yb[:, WIDTH_B:2 * WIDTH_B], gkb_ref)
    vb_ref[0] = yb[:, 2 * WIDTH_B:].astype(BF16)


def _proj(x, g1, wa, wb, ga, cos_t, sin_t, gqb, gkb):
    b, s, d = x.shape
    tm = PROJ_TM
    full = lambda shape: pl.BlockSpec(shape, lambda bi, i: (0,) * len(shape))
    out_shape = (
        jax.ShapeDtypeStruct((b, N_HEADS_A, 2 * HEAD_DIM, s), BF16),
        jax.ShapeDtypeStruct((b, s, KV_WIDTH_A), BF16),
        jax.ShapeDtypeStruct((b, KV_WIDTH_A, s), BF16),
        jax.ShapeDtypeStruct((b, s, WIDTH_B), BF16),
        jax.ShapeDtypeStruct((b, s, WIDTH_B), BF16),
        jax.ShapeDtypeStruct((b, s, WIDTH_B), BF16),
    )
    row = lambda w: pl.BlockSpec((1, tm, w), lambda bi, i: (bi, i, 0))
    return pl.pallas_call(
        _proj_kernel,
        grid=(b, s // tm),
        in_specs=[
            row(d), full(g1.shape), full(wa.shape), full(wb.shape), full(ga.shape),
            pl.BlockSpec((HEAD_DIM, tm), lambda bi, i: (0, i)),
            pl.BlockSpec((HEAD_DIM, tm), lambda bi, i: (0, i)),
            full(gqb.shape), full(gkb.shape),
        ],
        out_specs=(
            pl.BlockSpec((1, N_HEADS_A, 2 * HEAD_DIM, tm), lambda bi, i: (bi, 0, 0, i)),
            row(KV_WIDTH_A),
            pl.BlockSpec((1, KV_WIDTH_A, tm), lambda bi, i: (bi, 0, i)),
            row(WIDTH_B), row(WIDTH_B), row(WIDTH_B),
        ),
        out_shape=out_shape,
        compiler_params=pltpu.CompilerParams(
            dimension_semantics=("parallel", "parallel"), vmem_limit_bytes=VMEM_LIMIT_BYTES),
        name="proj",
    )(x, g1, wa, wb, ga, cos_t, sin_t, gqb, gkb)


def _attn_a_kernel(qt_ref, k_ref, vt_ref, ot_ref, m_ref, l_ref, acc_ref):
    j = pl.program_id(3)

    @pl.when(j == 0)
    def _():
        m_ref[...] = jnp.full(m_ref.shape, NEG_INF, F32)
        l_ref[...] = jnp.zeros(l_ref.shape, F32)
        acc_ref[...] = jnp.zeros(acc_ref.shape, F32)

    k = k_ref[0]
    vt = vt_ref[0]
    for h in range(GQA_GROUP):
        s_t = jnp.dot(k, qt_ref[0, h], preferred_element_type=F32)
        m_prev = m_ref[h]
        m_new = jnp.maximum(m_prev, jnp.max(s_t, axis=0, keepdims=True))
        alpha = jnp.exp(m_prev - m_new)
        p_t = jnp.exp(s_t - m_new)
        l_ref[h] = alpha * l_ref[h] + jnp.sum(p_t, axis=0, keepdims=True)
        acc_ref[h] = alpha * acc_ref[h] + jnp.dot(vt, p_t.astype(BF16), preferred_element_type=F32)
        m_ref[h] = m_new

    @pl.when(j == pl.num_programs(3) - 1)
    def _():
        for h in range(GQA_GROUP):
            ot_ref[0, HEAD_DIM * h:HEAD_DIM * (h + 1), :] = (acc_ref[h] * (1.0 / l_ref[h])).astype(BF16)


def _attn_a(qt, k, vt):
    b, _, _, s = qt.shape
    tq, tk = ATT_TQ, ATT_TK
    gw = GQA_GROUP * HEAD_DIM
    return pl.pallas_call(
        _attn_a_kernel,
        grid=(b, N_KV_A, s // tq, s // tk),
        in_specs=[
            pl.BlockSpec((1, GQA_GROUP, 2 * HEAD_DIM, tq), lambda bi, g, i, j: (bi, g, 0, i)),
            pl.BlockSpec((1, tk, KV_WIDTH_A), lambda bi, g, i, j: (bi, j, 0)),
            pl.BlockSpec((1, HEAD_DIM, tk), lambda bi, g, i, j: (bi, g, j)),
        ],
        out_specs=pl.BlockSpec((1, gw, tq), lambda bi, g, i, j: (bi, g, i)),
        out_shape=jax.ShapeDtypeStruct((b, WIDTH_A, s), BF16),
        scratch_shapes=[
            pltpu.VMEM((GQA_GROUP, 1, tq), F32),
            pltpu.VMEM((GQA_GROUP, 1, tq), F32),
            pltpu.VMEM((GQA_GROUP, HEAD_DIM, tq), F32),
        ],
        compiler_params=pltpu.CompilerParams(
            dimension_semantics=("parallel", "parallel", "parallel", "arbitrary"),
            vmem_limit_bytes=VMEM_LIMIT_BYTES),
        name="attn_a",
    )(qt, k, vt)


def _dil_kernel(q_ref, kp_ref, kc_ref, kn_ref, vp_ref, vc_ref, vn_ref, o_ref, lse_ref,
                kext_ref, vext_ref, bias_ref, *, dilation, length):
    tl = q_ref.shape[1]
    tk = DIL_SUB + 2 * DIL_BLOCK
    bi, r, i = pl.program_id(0), pl.program_id(1), pl.program_id(2)

    @pl.when((bi == 0) & (r == 0) & (i == 0))
    def _():
        qi = lax.broadcasted_iota(jnp.int32, (DIL_SUB, tk), 0)
        kj = lax.broadcasted_iota(jnp.int32, (DIL_SUB, tk), 1)
        aoff = jnp.abs(kj - DIL_BLOCK - qi)
        dist = (aoff * dilation).astype(F32)
        for h in range(N_HEADS_B):
            slope = 2.0 ** (-8.0 * (h + 1) / N_HEADS_B)
            bias_ref[h] = jnp.where(aoff <= DIL_BLOCK, -slope * dist, NEG_INF)

    kext_ref[0:DIL_BLOCK] = kp_ref[0]
    kext_ref[DIL_BLOCK:DIL_BLOCK + tl] = kc_ref[0]
    kext_ref[DIL_BLOCK + tl:] = kn_ref[0]
    vext_ref[0:DIL_BLOCK] = vp_ref[0]
    vext_ref[DIL_BLOCK:DIL_BLOCK + tl] = vc_ref[0]
    vext_ref[DIL_BLOCK + tl:] = vn_ref[0]

    lane = lax.broadcasted_iota(jnp.int32, (DIL_SUB, LANES), 1)
    first_head = lane < HEAD_DIM
    lse_owner = lane // LSE_LANES
    key_lane = lax.broadcasted_iota(jnp.int32, (1, tk), 1)

    def sub_tile(st, carry):
        r0 = pl.multiple_of(st * DIL_SUB, DIL_SUB)
        kidx = i * tl + r0 - DIL_BLOCK + key_lane
        in_range = (kidx >= 0) & (kidx < length)
        lse_tile = jnp.zeros((DIL_SUB, LANES), F32)
        for hp in range(N_HEADS_B // 2):
            cols = slice(LANES * hp, LANES * (hp + 1))
            qp = q_ref[0, pl.ds(r0, DIL_SUB), cols]
            kp = kext_ref[pl.ds(r0, tk), cols]
            vp = vext_ref[pl.ds(r0, tk), cols]
            outs = []
            for e in range(2):
                h = 2 * hp + e
                own = first_head if e == 0 else jnp.logical_not(first_head)
                qm = jnp.where(own, qp, jnp.zeros_like(qp))
                s = _nt_dot(qm, kp) + bias_ref[h]
                s = jnp.where(in_range, s, NEG_INF)
                m = jnp.max(s, axis=-1, keepdims=True)
                p = jnp.exp(s - m)
                l = jnp.sum(p, axis=-1, keepdims=True)
                pv = jnp.dot(p.astype(BF16), vp, preferred_element_type=F32)
                outs.append(pv * (1.0 / l))
                lse_tile = jnp.where(lse_owner == h, m + jnp.log(l), lse_tile)
            o_ref[0, pl.ds(r0, DIL_SUB), cols] = jnp.where(first_head, outs[0], outs[1]).astype(BF16)
        lse_ref[0, pl.ds(r0, DIL_SUB), :] = lse_tile
        return carry

    lax.fori_loop(0, tl // DIL_SUB, sub_tile, 0)


def _dilated(q, k, v, dilation):
    b, s, w = q.shape
    length = s // dilation
    tl = min(DIL_TL, length)
    blocks_per_tile = tl // DIL_BLOCK
    n_blocks = length // DIL_BLOCK
    view = lambda a: a.reshape(b, length, dilation * w)
    qv, kv, vv = view(q), view(k), view(v)
    cur = pl.BlockSpec((1, tl, w), lambda bi, r, i: (bi, i, r))
    prev = pl.BlockSpec((1, DIL_BLOCK, w),
                        lambda bi, r, i: (bi, jnp.maximum(i * blocks_per_tile - 1, 0), r))
    nxt = pl.BlockSpec((1, DIL_BLOCK, w),
                       lambda bi, r, i: (bi, jnp.minimum((i + 1) * blocks_per_tile, n_blocks - 1), r))
    o, lse = pl.pallas_call(
        functools.partial(_dil_kernel, dilation=dilation, length=length),
        grid=(b, dilation, length // tl),
        in_specs=[cur, prev, cur, nxt, prev, cur, nxt],
        out_specs=(cur, pl.BlockSpec((1, tl, LANES), lambda bi, r, i: (bi, i, r))),
        out_shape=(jax.ShapeDtypeStruct(qv.shape, BF16),
                   jax.ShapeDtypeStruct((b, length, dilation * LANES), F32)),
        scratch_shapes=[
            pltpu.VMEM((tl + 2 * DIL_BLOCK, w), BF16),
            pltpu.VMEM((tl + 2 * DIL_BLOCK, w), BF16),
            pltpu.VMEM((N_HEADS_B, DIL_SUB, DIL_SUB + 2 * DIL_BLOCK), F32),
        ],
        compiler_params=pltpu.CompilerParams(
            dimension_semantics=("arbitrary", "arbitrary", "arbitrary"),
            vmem_limit_bytes=VMEM_LIMIT_BYTES),
        name=f"dil_{dilation}",
    )(qv, kv, kv, kv, vv, vv, vv)
    return o.reshape(b, s, w), lse.reshape(b, s, LANES)


def _merge_kernel(x_ref, oat_ref, o1_ref, o2_ref, o3_ref, l1_ref, l2_ref, l3_ref,
                  ga_ref, gb_ref, wa_ref, wb_ref, out_ref):
    def group_norm(y, g_ref):
        ms = jnp.mean(y * y, axis=-1, keepdims=True)
        return (y * lax.rsqrt(ms + NORM_EPS) * g_ref[...]).astype(BF16)

    oa = oat_ref[0].astype(F32).T

    l1, l2, l3 = l1_ref[0], l2_ref[0], l3_ref[0]
    mx = jnp.maximum(jnp.maximum(l1, l2), l3)
    e1, e2, e3 = jnp.exp(l1 - mx), jnp.exp(l2 - mx), jnp.exp(l3 - mx)
    inv = 1.0 / (e1 + e2 + e3)
    ri = lax.broadcasted_iota(jnp.int32, (LANES, WIDTH_B), 0)
    ci = lax.broadcasted_iota(jnp.int32, (LANES, WIDTH_B), 1)
    expand = jnp.where(ri == (ci // HEAD_DIM) * LSE_LANES, 1.0, 0.0).astype(BF16)

    def widen(wgt):
        hi = wgt.astype(BF16)
        lo = (wgt - hi.astype(F32)).astype(BF16)
        return (jnp.dot(hi, expand, preferred_element_type=F32)
                + jnp.dot(lo, expand, preferred_element_type=F32))

    ob = (widen(e1 * inv) * o1_ref[0].astype(F32)
          + widen(e2 * inv) * o2_ref[0].astype(F32)
          + widen(e3 * inv) * o3_ref[0].astype(F32))
    mixed_a = group_norm(oa, ga_ref)
    mixed_b = group_norm(ob, gb_ref)
    out_ref[0] = (x_ref[0]
                  + jnp.dot(mixed_a, wa_ref[...], preferred_element_type=F32)
                  + jnp.dot(mixed_b, wb_ref[...], preferred_element_type=F32))


def _merge(x, oat, obs, lses, ga, gb, wo_a, wo_b):
    b, s, d = x.shape
    tm = MERGE_TM
    full = lambda shape: pl.BlockSpec(shape, lambda bi, i: (0,) * len(shape))
    row = lambda w: pl.BlockSpec((1, tm, w), lambda bi, i: (bi, i, 0))
    return pl.pallas_call(
        _merge_kernel,
        grid=(b, s // tm),
        in_specs=[row(d), pl.BlockSpec((1, WIDTH_A, tm), lambda bi, i: (bi, 0, i)),
                  row(WIDTH_B), row(WIDTH_B), row(WIDTH_B), row(LANES), row(LANES), row(LANES),
                  full(ga.shape), full(gb.shape), full(wo_a.shape), full(wo_b.shape)],
        out_specs=row(d),
        out_shape=jax.ShapeDtypeStruct((b, s, d), F32),
        compiler_params=pltpu.CompilerParams(
            dimension_semantics=("parallel", "parallel"), vmem_limit_bytes=VMEM_LIMIT_BYTES),
        name="merge",
    )(x, oat, *obs, *lses, ga, gb, wo_a, wo_b)


def _gelu_tanh(x):
    c = math.sqrt(2.0 / math.pi)
    return 0.5 * x * (1.0 + jnp.tanh(c * (x + 0.044715 * (x * x * x))))


def _ffn_kernel(xp_ref, xc_ref, xn_ref, g2_ref, wup_ref, cw_ref, cb_ref, wdn_ref, out_ref,
                hext_ref, act_ref, *, d_ff):
    tm = xc_ref.shape[1]
    halo = xp_ref.shape[1]
    i = pl.program_id(1)
    rows = tm + 2 * halo

    def normed(x):
        ms = jnp.mean(x * x, axis=-1, keepdims=True)
        return x * lax.rsqrt(ms + NORM_EPS) * g2_ref[...]

    keep_prev = (i > 0).astype(F32)
    keep_next = (i < pl.num_programs(1) - 1).astype(F32)
    hext_ref[0:halo] = (normed(xp_ref[0]) * keep_prev).astype(BF16)
    hext_ref[halo:halo + tm] = normed(xc_ref[0]).astype(BF16)
    hext_ref[halo + tm:] = (normed(xn_ref[0]) * keep_next).astype(BF16)
    hext = hext_ref[...]

    def conv(u, col):
        w = cw_ref[:, col:col + FFN_FC]
        up = pltpu.roll(u, 1, 0)
        dn = pltpu.roll(u, rows - 1, 0)
        y = up * w[0:1] + u * w[1:2] + dn * w[2:3] + cb_ref[:, col:col + FFN_FC]
        return y[halo:halo + tm]

    for c in range(0, d_ff, FFN_FC):
        ug = jnp.dot(hext, wup_ref[:, c:c + FFN_FC], preferred_element_type=F32)
        uv = jnp.dot(hext, wup_ref[:, d_ff + c:d_ff + c + FFN_FC], preferred_element_type=F32)
        act_ref[:, c:c + FFN_FC] = (_gelu_tanh(conv(ug, c)) * conv(uv, d_ff + c)).astype(BF16)

    out_ref[0] = xc_ref[0] + jnp.dot(act_ref[...], wdn_ref[...], preferred_element_type=F32)


def _ffn(x, g2, w_up, conv_w, conv_b, w_down):
    b, s, d = x.shape
    d_ff = w_down.shape[0]
    tm, halo = FFN_TM, FFN_HALO
    per_tile = tm // halo
    n_halo_blocks = s // halo
    full = lambda shape: pl.BlockSpec(shape, lambda bi, i: (0,) * len(shape))
    return pl.pallas_call(
        functools.partial(_ffn_kernel, d_ff=d_ff),
        grid=(b, s // tm),
        in_specs=[
            pl.BlockSpec((1, halo, d), lambda bi, i: (bi, jnp.maximum(i * per_tile - 1, 0), 0)),
            pl.BlockSpec((1, tm, d), lambda bi, i: (bi, i, 0)),
            pl.BlockSpec((1, halo, d), lambda bi, i: (bi, jnp.minimum((i + 1) * per_tile, n_halo_blocks - 1), 0)),
            full(g2.shape), full(w_up.shape), full(conv_w.shape), full(conv_b.shape), full(w_down.shape),
        ],
        out_specs=pl.BlockSpec((1, tm, d), lambda bi, i: (bi, i, 0)),
        out_shape=jax.ShapeDtypeStruct((b, s, d), F32),
        scratch_shapes=[pltpu.VMEM((tm + 2 * halo, d), BF16), pltpu.VMEM((tm, d_ff), BF16)],
        compiler_params=pltpu.CompilerParams(
            dimension_semantics=("parallel", "parallel"), vmem_limit_bytes=VMEM_LIMIT_BYTES),
        name="ffn",
    )(x, x, x, g2, w_up, conv_w, conv_b, w_down)


def _rope_tables_t(seq_len):
    rows = seq_len // GRID_W
    row = jnp.repeat(jnp.arange(rows, dtype=F32), GRID_W)
    col = jnp.tile(jnp.arange(GRID_W, dtype=F32), rows)
    inv = ROPE_THETA ** (-jnp.arange(0, ROPE_AXIS_DIM, 2, dtype=F32) / ROPE_AXIS_DIM)
    ang_r = (row[:, None] * inv[None, :]).T
    ang_c = (col[:, None] * inv[None, :]).T
    cos_t = jnp.concatenate([jnp.cos(ang_r)] * 2 + [jnp.cos(ang_c)] * 2, axis=0)
    sin_t = jnp.concatenate([-jnp.sin(ang_r), jnp.sin(ang_r), -jnp.sin(ang_c), jnp.sin(ang_c)], axis=0)
    return cos_t, sin_t


def kernel(x, norm1_g, w_in, qa_norm_g, ka_norm_g, qb_norm_g, kb_norm_g, outa_norm_g, outb_norm_g,
           w_out, norm2_g, w_up, conv_w, conv_b, w_down):
    b, s, d = x.shape
    scale = HEAD_DIM ** -0.5
    a_cols = WIDTH_A + 2 * KV_WIDTH_A
    wa = w_in[:, :a_cols].astype(BF16)
    wb = w_in[:, a_cols:].astype(BF16)
    ga = jnp.concatenate([jnp.tile(qa_norm_g, N_HEADS_A) * scale, jnp.tile(ka_norm_g, N_KV_A)])
    ga = jnp.broadcast_to(ga[:, None], (ga.shape[0], LANES))
    gqb = (jnp.tile(qb_norm_g, N_HEADS_B) * scale)[None, :]
    gkb = jnp.tile(kb_norm_g, N_HEADS_B)[None, :]
    cos_t, sin_t = _rope_tables_t(s)

    qt, k, vt, qb, kb, vb = _proj(x, norm1_g[None, :], wa, wb, ga, cos_t, sin_t, gqb, gkb)
    oat = _attn_a(qt, k, vt)
    obs, lses = zip(*[_dilated(qb, kb, vb, dil) for _, dil in DILATED_PATTERNS])
    wo = w_out.astype(BF16)
    x2 = _merge(x, oat, obs, lses, outa_norm_g[None, :], outb_norm_g[None, :], wo[:WIDTH_A], wo[WIDTH_A:])
    return _ffn(x2, norm2_g[None, :], w_up.astype(BF16), conv_w, conv_b[None, :], w_down.astype(BF16))
```

```python
import functools
import math

import jax
import jax.numpy as jnp
from jax import lax
from jax.experimental import pallas as pl
from jax.experimental.pallas import tpu as pltpu

F32 = jnp.float32
BF16 = jnp.bfloat16

HEAD_DIM = 64
N_HEADS_A = 8
N_KV_A = 2
GQA_GROUP = N_HEADS_A // N_KV_A
N_HEADS_B = 8
WIDTH_A = N_HEADS_A * HEAD_DIM
WIDTH_B = N_HEADS_B * HEAD_DIM
KV_WIDTH_A = N_KV_A * HEAD_DIM
GRID_W = 64
ROPE_THETA = 10000.0
ROPE_AXIS_DIM = HEAD_DIM // 2
DIL_BLOCK = 64
DILATED_PATTERNS = ((128, 1), (512, 4), (2048, 16))
NORM_EPS = 1e-6
NEG_INF = -1e30

LANES = 128
MXU_DIM = 256
VMEM_LIMIT_BYTES = 56 * 1024 * 1024

PROJ_TM = 512
GROUP_STEP = 4
ATT_TQ = 1024
ATT_TK = 1024
VT_ROWS = HEAD_DIM + 16
SAFE_SCORE_BOUND = 40.0
DIL_TL = 512
DIL_SUB = 128
FFN_TM = 512
FFN_FC = 256
FFN_HALO = 16
LSE_LANES = 16


def _nt_dot(a, b):
    return lax.dot_general(a, b, (((1,), (1,)), ((), ())), preferred_element_type=F32)


def _proj_kernel(x_ref, g1_ref, wa_ref, wb_ref, ga_ref, cos_ref, sin_ref, gqb_ref, gkb_ref,
                 qt_ref, k_ref, vt_ref, qb_ref, kb_ref, vb_ref, *rest):
    *grouped_refs, nat_ref, lvl_ref = rest
    tm = x_ref.shape[1]
    x = x_ref[0]
    ms = jnp.mean(x * x, axis=-1, keepdims=True)
    hn = (x * lax.rsqrt(ms + NORM_EPS) * g1_ref[...]).astype(BF16)
    ya = jnp.dot(hn, wa_ref[...], preferred_element_type=F32)
    yb = jnp.dot(hn, wb_ref[...], preferred_element_type=F32)

    ya_t = ya.T
    ones_rows = jnp.where(lax.broadcasted_iota(jnp.int32, (VT_ROWS - HEAD_DIM, tm), 0) == 0, 1.0, 0.0).astype(BF16)
    for g in range(N_KV_A):
        v_rows = ya_t[WIDTH_A + KV_WIDTH_A + HEAD_DIM * g:WIDTH_A + KV_WIDTH_A + HEAD_DIM * (g + 1)]
        vt_ref[0, g] = jnp.concatenate([v_rows.astype(BF16), ones_rows], axis=0)
    cos_t = cos_ref[...]
    sin_t = sin_ref[...]
    reps = tm // LANES
    zeros = jnp.zeros((HEAD_DIM, tm), BF16)
    k_parts = []
    q16 = ROPE_AXIS_DIM // 2
    for h in range(N_HEADS_A + N_KV_A):
        u = ya_t[HEAD_DIM * h:HEAD_DIM * (h + 1)]
        r = lax.rsqrt(jnp.mean(u * u, axis=0, keepdims=True) + NORM_EPS)
        g = jnp.tile(ga_ref[HEAD_DIM * h:HEAD_DIM * (h + 1), :], (1, reps))
        un = u * r * g
        partner = jnp.concatenate(
            [un[q16:2 * q16], un[0:q16], un[3 * q16:4 * q16], un[2 * q16:3 * q16]], axis=0)
        o = un * cos_t + partner * sin_t
        if h < N_HEADS_A:
            ob = o.astype(BF16)
            pieces = [ob, zeros] if h // GQA_GROUP == 0 else [zeros, ob]
            qt_ref[0, h] = jnp.concatenate(pieces, axis=0)
        else:
            k_parts.append(o)
    k_ref[0] = jnp.concatenate(k_parts, axis=0).T.astype(BF16)

    ri = lax.broadcasted_iota(jnp.int32, (MXU_DIM, MXU_DIM), 0) // HEAD_DIM
    ci = lax.broadcasted_iota(jnp.int32, (MXU_DIM, MXU_DIM), 1) // HEAD_DIM
    ones_bd = jnp.where(ri == ci, 1.0, 0.0).astype(BF16)

    def head_norm(y, g_ref):
        sq = (y * y).astype(BF16)
        ss = jnp.concatenate(
            [jnp.dot(sq[:, c:c + MXU_DIM], ones_bd, preferred_element_type=F32)
             for c in range(0, y.shape[1], MXU_DIM)], axis=1)
        return y * lax.rsqrt(ss * (1.0 / HEAD_DIM) + NORM_EPS) * g_ref[...]

    parts = (head_norm(yb[:, :WIDTH_B], gqb_ref), head_norm(yb[:, WIDTH_B:2 * WIDTH_B], gkb_ref),
             yb[:, 2 * WIDTH_B:])
    slabs = WIDTH_B // LANES
    for c, (part, ref) in enumerate(zip(parts, (qb_ref, kb_ref, vb_ref))):
        ref[0] = part.astype(BF16)
        for jj in range(slabs):
            nat_ref[slabs * c + jj] = part[:, LANES * jj:LANES * (jj + 1)]

    src_ref, block = nat_ref, tm
    n_levels = len(grouped_refs) // 3
    for n in range(n_levels):
        sub = block // GROUP_STEP
        for j in range(3 * slabs):
            c, jj = divmod(j, slabs)
            dst_ref = grouped_refs[3 * n + c]
            for lo in range(0, tm, sub):
                blk0, r = (lo // block) * block, (lo % block) // sub
                rows = src_ref[j, pl.ds(blk0 + r, sub, stride=GROUP_STEP), :]
                if n + 1 < n_levels:
                    lvl_ref[j, lo:lo + sub, :] = rows
                dst_ref[0, lo:lo + sub, LANES * jj:LANES * (jj + 1)] = rows.astype(BF16)
        src_ref, block = lvl_ref, sub


def _group_index(tau, tm, dilation):
    base, size, local = 0, tm, tau
    while dilation > 1:
        size //= GROUP_STEP
        base = base + (local % GROUP_STEP) * size
        local = local // GROUP_STEP
        dilation //= GROUP_STEP
    return base + local


def _proj(x, g1, wa, wb, ga, cos_t, sin_t, gqb, gkb, grouped_dilations):
    b, s, d = x.shape
    tm = PROJ_TM
    assert tuple(grouped_dilations) == tuple(GROUP_STEP ** (n + 1) for n in range(len(grouped_dilations)))
    assert len(grouped_dilations) <= 2
    full = lambda shape: pl.BlockSpec(shape, lambda bi, i: (0,) * len(shape))
    rows_b = jax.ShapeDtypeStruct((b, s, WIDTH_B), BF16)
    out_shape = (
        jax.ShapeDtypeStruct((b, N_HEADS_A, 2 * HEAD_DIM, s), BF16),
        jax.ShapeDtypeStruct((b, s, KV_WIDTH_A), BF16),
        jax.ShapeDtypeStruct((b, N_KV_A, VT_ROWS, s), BF16),
    ) + (rows_b,) * (3 + 3 * len(grouped_dilations))
    row = lambda w: pl.BlockSpec((1, tm, w), lambda bi, i: (bi, i, 0))
    return pl.pallas_call(
        _proj_kernel,
        grid=(b, s // tm),
        in_specs=[
            row(d), full(g1.shape), full(wa.shape), full(wb.shape), full(ga.shape),
            pl.BlockSpec((HEAD_DIM, tm), lambda bi, i: (0, i)),
            pl.BlockSpec((HEAD_DIM, tm), lambda bi, i: (0, i)),
            full(gqb.shape), full(gkb.shape),
        ],
        out_specs=(
            pl.BlockSpec((1, N_HEADS_A, 2 * HEAD_DIM, tm), lambda bi, i: (bi, 0, 0, i)),
            row(KV_WIDTH_A),
            pl.BlockSpec((1, N_KV_A, VT_ROWS, tm), lambda bi, i: (bi, 0, 0, i)),
        ) + (row(WIDTH_B),) * (3 + 3 * len(grouped_dilations)),
        out_shape=out_shape,
        scratch_shapes=[pltpu.VMEM((3 * WIDTH_B // LANES, tm, LANES), F32)] * 2,
        compiler_params=pltpu.CompilerParams(
            dimension_semantics=("parallel", "parallel"), vmem_limit_bytes=VMEM_LIMIT_BYTES),
        name="proj",
    )(x, g1, wa, wb, ga, cos_t, sin_t, gqb, gkb)


def _attn_a_kernel(bounded_ref, qt_ref, k_ref, vt_ref, ot_ref, m_ref, acc_ref):
    j = pl.program_id(3)
    bounded = bounded_ref[0] != 0

    @pl.when(j == 0)
    def _():
        m_ref[...] = jnp.full(m_ref.shape, NEG_INF, F32)
        acc_ref[...] = jnp.zeros(acc_ref.shape, F32)

    k = k_ref[0]
    vt = vt_ref[0, 0]

    @pl.when(bounded)
    def _():
        for h in range(GQA_GROUP):
            s_t = jnp.dot(k, qt_ref[0, h], preferred_element_type=F32)
            acc_ref[h] += jnp.dot(vt, jnp.exp2(s_t).astype(BF16), preferred_element_type=F32)

    @pl.when(jnp.logical_not(bounded))
    def _():
        for h in range(GQA_GROUP):
            s_t = jnp.dot(k, qt_ref[0, h], preferred_element_type=F32)
            m_prev = m_ref[h]
            m_new = jnp.maximum(m_prev, jnp.max(s_t, axis=0, keepdims=True))
            p_t = jnp.exp2(s_t - m_new).astype(BF16)
            acc_ref[h] = (jnp.exp2(m_prev - m_new) * acc_ref[h]
                          + jnp.dot(vt, p_t, preferred_element_type=F32))
            m_ref[h] = m_new

    @pl.when(j == pl.num_programs(3) - 1)
    def _():
        for h in range(GQA_GROUP):
            a = acc_ref[h]
            ot_ref[0, HEAD_DIM * h:HEAD_DIM * (h + 1), :] = (
                a[:HEAD_DIM] * (1.0 / a[HEAD_DIM:HEAD_DIM + 1])).astype(BF16)


def _attn_a(bounded, qt, k, vt):
    b, _, _, s = qt.shape
    tq, tk = ATT_TQ, ATT_TK
    gw = GQA_GROUP * HEAD_DIM
    return pl.pallas_call(
        _attn_a_kernel,
        grid=(b, N_KV_A, s // tq, s // tk),
        in_specs=[
            pl.BlockSpec(memory_space=pltpu.SMEM),
            pl.BlockSpec((1, GQA_GROUP, 2 * HEAD_DIM, tq), lambda bi, g, i, j: (bi, g, 0, i)),
            pl.BlockSpec((1, tk, KV_WIDTH_A), lambda bi, g, i, j: (bi, j, 0)),
            pl.BlockSpec((1, 1, VT_ROWS, tk), lambda bi, g, i, j: (bi, g, 0, j)),
        ],
        out_specs=pl.BlockSpec((1, gw, tq), lambda bi, g, i, j: (bi, g, i)),
        out_shape=jax.ShapeDtypeStruct((b, WIDTH_A, s), BF16),
        scratch_shapes=[
            pltpu.VMEM((GQA_GROUP, 1, tq), F32),
            pltpu.VMEM((GQA_GROUP, VT_ROWS, tq), F32),
        ],
        compiler_params=pltpu.CompilerParams(
            dimension_semantics=("parallel", "parallel", "parallel", "arbitrary"),
            vmem_limit_bytes=VMEM_LIMIT_BYTES),
        name="attn_a",
    )(bounded, qt, k, vt)


def _dil_bias_init(bias_ref, dilation):
    tk = DIL_SUB + 2 * DIL_BLOCK
    qi = lax.broadcasted_iota(jnp.int32, (DIL_SUB, tk), 0)
    kj = lax.broadcasted_iota(jnp.int32, (DIL_SUB, tk), 1)
    aoff = jnp.abs(kj - DIL_BLOCK - qi)
    dist = (aoff * dilation).astype(F32)
    for h in range(N_HEADS_B):
        slope = 2.0 ** (-8.0 * (h + 1) / N_HEADS_B) * math.log2(math.e)
        bias_ref[h] = jnp.where(aoff <= DIL_BLOCK, -slope * dist, NEG_INF)


def _dil_sub_tiles(bounded, q_ref, kext_ref, vext_ref, o_ref, lse_ref, bias_ref, *, n_sub, first_key, length):
    tk = DIL_SUB + 2 * DIL_BLOCK
    lane = lax.broadcasted_iota(jnp.int32, (DIL_SUB, LANES), 1)
    first_head = lane < HEAD_DIM
    lse_owner = lane // LSE_LANES
    key_lane = lax.broadcasted_iota(jnp.int32, (1, tk), 1)

    def sub_tile(st, carry, *, use_max):
        r0 = pl.multiple_of(st * DIL_SUB, DIL_SUB)
        kidx = first_key + r0 + key_lane
        in_range = (kidx >= 0) & (kidx < length)
        lse_tile = jnp.zeros((DIL_SUB, LANES), F32)
        for hp in range(N_HEADS_B // 2):
            cols = slice(LANES * hp, LANES * (hp + 1))
            qp = q_ref[pl.ds(r0, DIL_SUB), cols]
            kp = kext_ref[pl.ds(r0, tk), cols]
            vp = vext_ref[pl.ds(r0, tk), cols]
            outs = []
            for e in range(2):
                h = 2 * hp + e
                own = first_head if e == 0 else jnp.logical_not(first_head)
                qm = jnp.where(own, qp, jnp.zeros_like(qp))
                s = _nt_dot(qm, kp) + bias_ref[h]
                s = jnp.where(in_range, s, NEG_INF)
                if use_max:
                    m = jnp.max(s, axis=-1, keepdims=True)
                    p = jnp.exp2(s - m)
                else:
                    p = jnp.exp2(s)
                l = jnp.sum(p, axis=-1, keepdims=True)
                pv = jnp.dot(p.astype(BF16), vp, preferred_element_type=F32)
                outs.append(pv * (1.0 / l))
                lse = jnp.log(l) + m * math.log(2.0) if use_max else jnp.log(l)
                lse_tile = jnp.where(lse_owner == h, lse, lse_tile)
            o_ref[pl.ds(r0, DIL_SUB), cols] = jnp.where(first_head, outs[0], outs[1]).astype(BF16)
        lse_ref[pl.ds(r0, DIL_SUB), :] = lse_tile
        return carry

    @pl.when(bounded)
    def _():
        lax.fori_loop(0, n_sub, functools.partial(sub_tile, use_max=False), 0)

    @pl.when(jnp.logical_not(bounded))
    def _():
        lax.fori_loop(0, n_sub, functools.partial(sub_tile, use_max=True), 0)


def _dil_rows_kernel(bounded_ref, q_ref, kp_ref, kc_ref, kn_ref, vp_ref, vc_ref, vn_ref, o_ref, lse_ref,
                     kext_ref, vext_ref, bias_ref, *, length):
    tl = q_ref.shape[1]
    bi, i = pl.program_id(0), pl.program_id(1)

    @pl.when((bi == 0) & (i == 0))
    def _():
        _dil_bias_init(bias_ref, 1)

    kext_ref[0:DIL_BLOCK] = kp_ref[0]
    kext_ref[DIL_BLOCK:DIL_BLOCK + tl] = kc_ref[0]
    kext_ref[DIL_BLOCK + tl:] = kn_ref[0]
    vext_ref[0:DIL_BLOCK] = vp_ref[0]
    vext_ref[DIL_BLOCK:DIL_BLOCK + tl] = vc_ref[0]
    vext_ref[DIL_BLOCK + tl:] = vn_ref[0]
    _dil_sub_tiles(bounded_ref[0] != 0, q_ref.at[0], kext_ref, vext_ref, o_ref.at[0], lse_ref.at[0], bias_ref,
                   n_sub=tl // DIL_SUB, first_key=i * tl - DIL_BLOCK, length=length)


def _dilated_rows(bounded, q, k, v):
    b, s, w = q.shape
    tl = DIL_TL
    blocks_per_tile = tl // DIL_BLOCK
    n_blocks = s // DIL_BLOCK
    cur = pl.BlockSpec((1, tl, w), lambda bi, i: (bi, i, 0))
    prev = pl.BlockSpec((1, DIL_BLOCK, w), lambda bi, i: (bi, jnp.maximum(i * blocks_per_tile - 1, 0), 0))
    nxt = pl.BlockSpec((1, DIL_BLOCK, w),
                       lambda bi, i: (bi, jnp.minimum((i + 1) * blocks_per_tile, n_blocks - 1), 0))
    return pl.pallas_call(
        functools.partial(_dil_rows_kernel, length=s),
        grid=(b, s // tl),
        in_specs=[pl.BlockSpec(memory_space=pltpu.SMEM), cur, prev, cur, nxt, prev, cur, nxt],
        out_specs=(cur, pl.BlockSpec((1, tl, LANES), lambda bi, i: (bi, i, 0))),
        out_shape=(jax.ShapeDtypeStruct((b, s, w), BF16), jax.ShapeDtypeStruct((b, s, LANES), F32)),
        scratch_shapes=[
            pltpu.VMEM((tl + 2 * DIL_BLOCK, w), BF16),
            pltpu.VMEM((tl + 2 * DIL_BLOCK, w), BF16),
            pltpu.VMEM((N_HEADS_B, DIL_SUB, DIL_SUB + 2 * DIL_BLOCK), F32),
        ],
        compiler_params=pltpu.CompilerParams(
            dimension_semantics=("arbitrary", "arbitrary"), vmem_limit_bytes=VMEM_LIMIT_BYTES),
        name="dil_1",
    )(bounded, q, k, k, k, v, v, v)


def _dil_grouped_kernel(bounded_ref, q_ref, k_ref, v_ref, o_ref, lse_ref,
                        qflat_ref, kext_ref, vext_ref, oflat_ref, lseflat_ref, bias_ref, *, dilation):
    n_tiles, rows = q_ref.shape[1], q_ref.shape[2]
    length = n_tiles * rows
    bi, r = pl.program_id(0), pl.program_id(1)

    @pl.when((bi == 0) & (r == 0))
    def _():
        _dil_bias_init(bias_ref, dilation)

    border = jnp.zeros((DIL_BLOCK, kext_ref.shape[1]), BF16)
    for ext_ref in (kext_ref, vext_ref):
        ext_ref[0:DIL_BLOCK] = border
        ext_ref[DIL_BLOCK + length:] = border
    for t in range(n_tiles):
        qflat_ref[rows * t:rows * (t + 1)] = q_ref[0, t]
        kext_ref[DIL_BLOCK + rows * t:DIL_BLOCK + rows * (t + 1)] = k_ref[0, t]
        vext_ref[DIL_BLOCK + rows * t:DIL_BLOCK + rows * (t + 1)] = v_ref[0, t]
    _dil_sub_tiles(bounded_ref[0] != 0, qflat_ref, kext_ref, vext_ref, oflat_ref, lseflat_ref, bias_ref,
                   n_sub=length // DIL_SUB, first_key=-DIL_BLOCK, length=length)
    for t in range(n_tiles):
        o_ref[0, t] = oflat_ref[rows * t:rows * (t + 1)]
        lse_ref[0, t] = lseflat_ref[rows * t:rows * (t + 1)]


def _dilated_grouped(bounded, q, k, v, dilation):
    b, s, w = q.shape
    n_tiles, rows = s // PROJ_TM, PROJ_TM // dilation
    length = n_tiles * rows
    view = lambda a: a.reshape(b, n_tiles, dilation, rows, a.shape[-1])
    spec = lambda width: pl.BlockSpec((1, n_tiles, None, rows, width), lambda bi, r: (bi, 0, r, 0, 0))
    o, lse = pl.pallas_call(
        functools.partial(_dil_grouped_kernel, dilation=dilation),
        grid=(b, dilation),
        in_specs=[pl.BlockSpec(memory_space=pltpu.SMEM), spec(w), spec(w), spec(w)],
        out_specs=(spec(w), spec(LANES)),
        out_shape=(jax.ShapeDtypeStruct((b, n_tiles, dilation, rows, w), BF16),
                   jax.ShapeDtypeStruct((b, n_tiles, dilation, rows, LANES), F32)),
        scratch_shapes=[
            pltpu.VMEM((length, w), BF16),
            pltpu.VMEM((length + 2 * DIL_BLOCK, w), BF16),
            pltpu.VMEM((length + 2 * DIL_BLOCK, w), BF16),
            pltpu.VMEM((length, w), BF16),
            pltpu.VMEM((length, LANES), F32),
            pltpu.VMEM((N_HEADS_B, DIL_SUB, DIL_SUB + 2 * DIL_BLOCK), F32),
        ],
        compiler_params=pltpu.CompilerParams(
            dimension_semantics=("arbitrary", "arbitrary"), vmem_limit_bytes=VMEM_LIMIT_BYTES),
        name=f"dil_{dilation}",
    )(bounded, view(q), view(k), view(v))
    return o.reshape(b, s, w), lse.reshape(b, s, LANES)


def _merge_kernel(x_ref, oat_ref, o1_ref, o2_ref, o3_ref, l1_ref, l2_ref, l3_ref,
                  ga_ref, gb_ref, wa_ref, wb_ref, unperm_ref, out_ref):
    def group_norm(y, g_ref):
        ms = jnp.mean(y * y, axis=-1, keepdims=True)
        return (y * lax.rsqrt(ms + NORM_EPS) * g_ref[...]).astype(BF16)

    oa = oat_ref[0].astype(F32).T

    def ungroup(n, val):
        return jnp.dot(unperm_ref[n], val, preferred_element_type=F32)

    def limbs(val):
        hi = val.astype(BF16)
        return hi, (val - hi.astype(F32)).astype(BF16)

    def ungroup_f32(n, val):
        both = ungroup(n, jnp.concatenate(limbs(val), axis=1))
        return both[:, :LANES] + both[:, LANES:]

    l1, l2, l3 = l1_ref[0], ungroup_f32(0, l2_ref[0]), ungroup_f32(1, l3_ref[0])
    mx = jnp.maximum(jnp.maximum(l1, l2), l3)
    e1, e2, e3 = jnp.exp(l1 - mx), jnp.exp(l2 - mx), jnp.exp(l3 - mx)
    inv = 1.0 / (e1 + e2 + e3)
    ri = lax.broadcasted_iota(jnp.int32, (2 * LANES, WIDTH_B), 0) % LANES
    ci = lax.broadcasted_iota(jnp.int32, (2 * LANES, WIDTH_B), 1)
    expand = jnp.where(ri == (ci // HEAD_DIM) * LSE_LANES, 1.0, 0.0).astype(BF16)

    def widen(wgt):
        return jnp.dot(jnp.concatenate(limbs(wgt), axis=1), expand, preferred_element_type=F32)

    ob = (widen(e1 * inv) * o1_ref[0].astype(F32)
          + widen(e2 * inv) * ungroup(0, o2_ref[0])
          + widen(e3 * inv) * ungroup(1, o3_ref[0]))
    mixed_a = group_norm(oa, ga_ref)
    mixed_b = group_norm(ob, gb_ref)
    out_ref[0] = (x_ref[0]
                  + jnp.dot(mixed_a, wa_ref[...], preferred_element_type=F32)
                  + jnp.dot(mixed_b, wb_ref[...], preferred_element_type=F32))


def _merge(x, oat, obs, lses, ga, gb, wo_a, wo_b, grouped_dilations):
    b, s, d = x.shape
    tm = PROJ_TM
    tau = jnp.arange(tm)
    unperm = jnp.stack([(jnp.arange(tm)[None, :] == _group_index(tau, tm, dil)[:, None]).astype(BF16)
                        for dil in grouped_dilations])
    full = lambda shape: pl.BlockSpec(shape, lambda bi, i: (0,) * len(shape))
    row = lambda w: pl.BlockSpec((1, tm, w), lambda bi, i: (bi, i, 0))
    return pl.pallas_call(
        _merge_kernel,
        grid=(b, s // tm),
        in_specs=[row(d), pl.BlockSpec((1, WIDTH_A, tm), lambda bi, i: (bi, 0, i)),
                  row(WIDTH_B), row(WIDTH_B), row(WIDTH_B), row(LANES), row(LANES), row(LANES),
                  full(ga.shape), full(gb.shape), full(wo_a.shape), full(wo_b.shape), full(unperm.shape)],
        out_specs=row(d),
        out_shape=jax.ShapeDtypeStruct((b, s, d), F32),
        compiler_params=pltpu.CompilerParams(
            dimension_semantics=("parallel", "parallel"), vmem_limit_bytes=VMEM_LIMIT_BYTES),
        name="merge",
    )(x, oat, *obs, *lses, ga, gb, wo_a, wo_b, unperm)


def _gelu_tanh(x):
    c = math.sqrt(2.0 / math.pi)
    return 0.5 * x * (1.0 + jnp.tanh(c * (x + 0.044715 * (x * x * x))))


def _ffn_kernel(xp_ref, xc_ref, xn_ref, g2_ref, wup_ref, cw_ref, cb_ref, wdn_ref, out_ref,
                hext_ref, act_ref, *, d_ff):
    tm = xc_ref.shape[1]
    halo = xp_ref.shape[1]
    i = pl.program_id(1)
    rows = tm + 2 * halo

    def normed(x):
        ms = jnp.mean(x * x, axis=-1, keepdims=True)
        return x * lax.rsqrt(ms + NORM_EPS) * g2_ref[...]

    keep_prev = (i > 0).astype(F32)
    keep_next = (i < pl.num_programs(1) - 1).astype(F32)
    hext_ref[0:halo] = (normed(xp_ref[0]) * keep_prev).astype(BF16)
    hext_ref[halo:halo + tm] = normed(xc_ref[0]).astype(BF16)
    hext_ref[halo + tm:] = (normed(xn_ref[0]) * keep_next).astype(BF16)
    hext = hext_ref[...]

    def conv(u, col):
        w = cw_ref[:, col:col + FFN_FC]
        up = pltpu.roll(u, 1, 0)
        dn = pltpu.roll(u, rows - 1, 0)
        y = up * w[0:1] + u * w[1:2] + dn * w[2:3] + cb_ref[:, col:col + FFN_FC]
        return y[halo:halo + tm]

    for c in range(0, d_ff, FFN_FC):
        ug = jnp.dot(hext, wup_ref[:, c:c + FFN_FC], preferred_element_type=F32)
        uv = jnp.dot(hext, wup_ref[:, d_ff + c:d_ff + c + FFN_FC], preferred_element_type=F32)
        act_ref[:, c:c + FFN_FC] = (_gelu_tanh(conv(ug, c)) * conv(uv, d_ff + c)).astype(BF16)

    out_ref[0] = xc_ref[0] + jnp.dot(act_ref[...], wdn_ref[...], preferred_element_type=F32)


def _ffn(x, g2, w_up, conv_w, conv_b, w_down):
    b, s, d = x.shape
    d_ff = w_down.shape[0]
    tm, halo = FFN_TM, FFN_HALO
    per_tile = tm // halo
    n_halo_blocks = s // halo
    full = lambda shape: pl.BlockSpec(shape, lambda bi, i: (0,) * len(shape))
    return pl.pallas_call(
        functools.partial(_ffn_kernel, d_ff=d_ff),
        grid=(b, s // tm),
        in_specs=[
            pl.BlockSpec((1, halo, d), lambda bi, i: (bi, jnp.maximum(i * per_tile - 1, 0), 0)),
            pl.BlockSpec((1, tm, d), lambda bi, i: (bi, i, 0)),
            pl.BlockSpec((1, halo, d), lambda bi, i: (bi, jnp.minimum((i + 1) * per_tile, n_halo_blocks - 1), 0)),
            full(g2.shape), full(w_up.shape), full(conv_w.shape), full(conv_b.shape), full(w_down.shape),
        ],
        out_specs=pl.BlockSpec((1, tm, d), lambda bi, i: (bi, i, 0)),
        out_shape=jax.ShapeDtypeStruct((b, s, d), F32),
        scratch_shapes=[pltpu.VMEM((tm + 2 * halo, d), BF16), pltpu.VMEM((tm, d_ff), BF16)],
        compiler_params=pltpu.CompilerParams(
            dimension_semantics=("parallel", "parallel"), vmem_limit_bytes=VMEM_LIMIT_BYTES),
        name="ffn",
    )(x, x, x, g2, w_up, conv_w, conv_b, w_down)


def _rope_tables_t(seq_len):
    rows = seq_len // GRID_W
    row = jnp.repeat(jnp.arange(rows, dtype=F32), GRID_W)
    col = jnp.tile(jnp.arange(GRID_W, dtype=F32), rows)
    inv = ROPE_THETA ** (-jnp.arange(0, ROPE_AXIS_DIM, 2, dtype=F32) / ROPE_AXIS_DIM)
    ang_r = (row[:, None] * inv[None, :]).T
    ang_c = (col[:, None] * inv[None, :]).T
    cos_t = jnp.concatenate([jnp.cos(ang_r)] * 2 + [jnp.cos(ang_c)] * 2, axis=0)
    sin_t = jnp.concatenate([-jnp.sin(ang_r), jnp.sin(ang_r), -jnp.sin(ang_c), jnp.sin(ang_c)], axis=0)
    return cos_t, sin_t


def kernel(x, norm1_g, w_in, qa_norm_g, ka_norm_g, qb_norm_g, kb_norm_g, outa_norm_g, outb_norm_g,
           w_out, norm2_g, w_up, conv_w, conv_b, w_down):
    b, s, d = x.shape
    scale = HEAD_DIM ** -0.5 * math.log2(math.e)
    a_cols = WIDTH_A + 2 * KV_WIDTH_A
    wa = w_in[:, :a_cols].astype(BF16)
    wb = w_in[:, a_cols:].astype(BF16)
    ga = jnp.concatenate([jnp.tile(qa_norm_g, N_HEADS_A) * scale, jnp.tile(ka_norm_g, N_KV_A)])
    ga = jnp.broadcast_to(ga[:, None], (ga.shape[0], LANES))
    gqb = (jnp.tile(qb_norm_g, N_HEADS_B) * scale)[None, :]
    gkb = jnp.tile(kb_norm_g, N_HEADS_B)[None, :]
    cos_t, sin_t = _rope_tables_t(s)

    dilations = [dil for _, dil in DILATED_PATTERNS]
    assert dilations[0] == 1
    grouped_dilations = dilations[1:]
    qt, k, vt, *qkv_b = _proj(x, norm1_g[None, :], wa, wb, ga, cos_t, sin_t, gqb, gkb, grouped_dilations)

    def scores_bounded(gq, gk):
        bound = math.sqrt(HEAD_DIM) * jnp.max(jnp.abs(gq)) * jnp.max(jnp.abs(gk))
        return (bound <= SAFE_SCORE_BOUND).astype(jnp.int32).reshape(1)

    oat = _attn_a(scores_bounded(qa_norm_g, ka_norm_g), qt, k, vt)
    bounded_b = scores_bounded(qb_norm_g, kb_norm_g)
    results = [_dilated_rows(bounded_b, *qkv_b[:3])]
    for n, dil in enumerate(grouped_dilations):
        results.append(_dilated_grouped(bounded_b, *qkv_b[3 * (n + 1):3 * (n + 2)], dil))
    obs, lses = zip(*results)
    wo = w_out.astype(BF16)
    x2 = _merge(x, oat, obs, lses, outa_norm_g[None, :], outb_norm_g[None, :], wo[:WIDTH_A], wo[WIDTH_A:],
                grouped_dilations)
    return _ffn(x2, norm2_g[None, :], w_up.astype(BF16), conv_w, conv_b[None, :], w_down.astype(BF16))
```

```python
import functools
import math

import jax
import jax.numpy as jnp
from jax import lax
from jax.experimental import pallas as pl
from jax.experimental.pallas import tpu as pltpu

F32 = jnp.float32
BF16 = jnp.bfloat16

HEAD_DIM = 64
N_HEADS_A = 8
N_KV_A = 2
GQA_GROUP = N_HEADS_A // N_KV_A
N_HEADS_B = 8
WIDTH_A = N_HEADS_A * HEAD_DIM
WIDTH_B = N_HEADS_B * HEAD_DIM
KV_WIDTH_A = N_KV_A * HEAD_DIM
GRID_W = 64
ROPE_THETA = 10000.0
ROPE_AXIS_DIM = HEAD_DIM // 2
DIL_BLOCK = 64
DILATED_PATTERNS = ((128, 1), (512, 4), (2048, 16))
NORM_EPS = 1e-6
NEG_INF = -1e30

LANES = 128
SUBLANES = 8
MXU_DIM = 256
VMEM_LIMIT_BYTES = 56 * 1024 * 1024

PROJ_TM = 512
GROUP_STEP = 4
ATT_TQ = 2048
ATT_TK = 1024
SAFE_SCORE_BOUND = 40.0
DIL_TL = 512
DIL_SUB = 128
DIL_UNROLL = 4
FFN_TM = 1024
FFN_FC = 256
FFN_HALO = 16
LSE_LANES = 16


def _nt_dot(a, b):
    return lax.dot_general(a, b, (((1,), (1,)), ((), ())), preferred_element_type=F32)


def _proj_kernel(x_ref, g1_ref, wa_ref, wb_ref, ga_ref, cos_ref, sin_ref, gqb_ref, gkb_ref,
                 qt_ref, k_ref, vt_ref, qb_ref, kb_ref, vb_ref, *rest):
    *grouped_refs, nat_ref, lvl_ref = rest
    tm = x_ref.shape[1]
    x = x_ref[0]
    ms = jnp.mean(x * x, axis=-1, keepdims=True)
    hn = (x * lax.rsqrt(ms + NORM_EPS) * g1_ref[...]).astype(BF16)
    ya = jnp.dot(hn, wa_ref[...], preferred_element_type=F32)
    yb = jnp.dot(hn, wb_ref[...], preferred_element_type=F32)

    ya_t = ya.T
    for g in range(N_KV_A):
        v_rows = ya_t[WIDTH_A + KV_WIDTH_A + HEAD_DIM * g:WIDTH_A + KV_WIDTH_A + HEAD_DIM * (g + 1)]
        vt_ref[0, g] = v_rows.astype(BF16)
    cos_t = cos_ref[...]
    sin_t = sin_ref[...]
    reps = tm // LANES
    zeros = jnp.zeros((HEAD_DIM, tm), BF16)
    k_parts = []
    q16 = ROPE_AXIS_DIM // 2
    for h in range(N_HEADS_A + N_KV_A):
        u = ya_t[HEAD_DIM * h:HEAD_DIM * (h + 1)]
        r = lax.rsqrt(jnp.mean(u * u, axis=0, keepdims=True) + NORM_EPS)
        g = jnp.tile(ga_ref[HEAD_DIM * h:HEAD_DIM * (h + 1), :], (1, reps))
        un = u * r * g
        partner = jnp.concatenate(
            [un[q16:2 * q16], un[0:q16], un[3 * q16:4 * q16], un[2 * q16:3 * q16]], axis=0)
        o = un * cos_t + partner * sin_t
        if h < N_HEADS_A:
            ob = o.astype(BF16)
            pieces = [ob, zeros] if h // GQA_GROUP == 0 else [zeros, ob]
            qt_ref[0, h] = jnp.concatenate(pieces, axis=0)
        else:
            k_parts.append(o)
    k_ref[0] = jnp.concatenate(k_parts, axis=0).T.astype(BF16)

    ri = lax.broadcasted_iota(jnp.int32, (MXU_DIM, MXU_DIM), 0) // HEAD_DIM
    ci = lax.broadcasted_iota(jnp.int32, (MXU_DIM, MXU_DIM), 1) // HEAD_DIM
    ones_bd = jnp.where(ri == ci, 1.0, 0.0).astype(BF16)

    def head_norm(y, g_ref):
        sq = (y * y).astype(BF16)
        ss = jnp.concatenate(
            [jnp.dot(sq[:, c:c + MXU_DIM], ones_bd, preferred_element_type=F32)
             for c in range(0, y.shape[1], MXU_DIM)], axis=1)
        return y * lax.rsqrt(ss * (1.0 / HEAD_DIM) + NORM_EPS) * g_ref[...]

    parts = (head_norm(yb[:, :WIDTH_B], gqb_ref), head_norm(yb[:, WIDTH_B:2 * WIDTH_B], gkb_ref),
             yb[:, 2 * WIDTH_B:])
    slabs = WIDTH_B // LANES
    for c, (part, ref) in enumerate(zip(parts, (qb_ref, kb_ref, vb_ref))):
        ref[0] = part.astype(BF16)
        for jj in range(slabs):
            nat_ref[slabs * c + jj] = part[:, LANES * jj:LANES * (jj + 1)]

    src_ref, block = nat_ref, tm
    n_levels = len(grouped_refs) // 3
    for n in range(n_levels):
        sub = block // GROUP_STEP
        for j in range(3 * slabs):
            c, jj = divmod(j, slabs)
            dst_ref = grouped_refs[3 * n + c]
            for lo in range(0, tm, sub):
                blk0, r = (lo // block) * block, (lo % block) // sub
                rows = src_ref[j, pl.ds(blk0 + r, sub, stride=GROUP_STEP), :]
                if n + 1 < n_levels:
                    lvl_ref[j, lo:lo + sub, :] = rows
                dst_ref[0, lo:lo + sub, LANES * jj:LANES * (jj + 1)] = rows.astype(BF16)
        src_ref, block = lvl_ref, sub


def _group_index(tau, tm, dilation):
    base, size, local = 0, tm, tau
    while dilation > 1:
        size //= GROUP_STEP
        base = base + (local % GROUP_STEP) * size
        local = local // GROUP_STEP
        dilation //= GROUP_STEP
    return base + local


def _proj(x, g1, wa, wb, ga, cos_t, sin_t, gqb, gkb, grouped_dilations):
    b, s, d = x.shape
    tm = PROJ_TM
    assert tuple(grouped_dilations) == tuple(GROUP_STEP ** (n + 1) for n in range(len(grouped_dilations)))
    assert len(grouped_dilations) <= 2
    full = lambda shape: pl.BlockSpec(shape, lambda bi, i: (0,) * len(shape))
    rows_b = jax.ShapeDtypeStruct((b, s, WIDTH_B), BF16)
    out_shape = (
        jax.ShapeDtypeStruct((b, N_HEADS_A, 2 * HEAD_DIM, s), BF16),
        jax.ShapeDtypeStruct((b, s, KV_WIDTH_A), BF16),
        jax.ShapeDtypeStruct((b, N_KV_A, HEAD_DIM, s), BF16),
    ) + (rows_b,) * (3 + 3 * len(grouped_dilations))
    row = lambda w: pl.BlockSpec((1, tm, w), lambda bi, i: (bi, i, 0))
    return pl.pallas_call(
        _proj_kernel,
        grid=(b, s // tm),
        in_specs=[
            row(d), full(g1.shape), full(wa.shape), full(wb.shape), full(ga.shape),
            pl.BlockSpec((HEAD_DIM, tm), lambda bi, i: (0, i)),
            pl.BlockSpec((HEAD_DIM, tm), lambda bi, i: (0, i)),
            full(gqb.shape), full(gkb.shape),
        ],
        out_specs=(
            pl.BlockSpec((1, N_HEADS_A, 2 * HEAD_DIM, tm), lambda bi, i: (bi, 0, 0, i)),
            row(KV_WIDTH_A),
            pl.BlockSpec((1, N_KV_A, HEAD_DIM, tm), lambda bi, i: (bi, 0, 0, i)),
        ) + (row(WIDTH_B),) * (3 + 3 * len(grouped_dilations)),
        out_shape=out_shape,
        scratch_shapes=[pltpu.VMEM((3 * WIDTH_B // LANES, tm, LANES), F32)] * 2,
        compiler_params=pltpu.CompilerParams(
            dimension_semantics=("parallel", "parallel"), vmem_limit_bytes=VMEM_LIMIT_BYTES),
        name="proj",
    )(x, g1, wa, wb, ga, cos_t, sin_t, gqb, gkb)


def _attn_a_kernel(bounded_ref, qt_ref, k_ref, vt_ref, ot_ref, m_ref, l_ref, acc_ref):
    j = pl.program_id(3)
    bounded = bounded_ref[0] != 0
    tk, tq = k_ref.shape[1], qt_ref.shape[3]

    @pl.when(j == 0)
    def _():
        m_ref[...] = jnp.full(m_ref.shape, NEG_INF, F32)
        l_ref[...] = jnp.zeros(l_ref.shape, F32)
        acc_ref[...] = jnp.zeros(acc_ref.shape, F32)

    k = k_ref[0]
    vt = vt_ref[0, 0]

    def key_partial_sums(p_t):
        return jnp.sum(p_t.reshape(tk // SUBLANES, SUBLANES, tq), axis=0)

    @pl.when(bounded)
    def _():
        for h in range(GQA_GROUP):
            s_t = jnp.dot(k, qt_ref[0, h], preferred_element_type=F32)
            p_t = jnp.exp2(s_t)
            l_ref[h] += key_partial_sums(p_t)
            acc_ref[h] += jnp.dot(vt, p_t.astype(BF16), preferred_element_type=F32)

    @pl.when(jnp.logical_not(bounded))
    def _():
        for h in range(GQA_GROUP):
            s_t = jnp.dot(k, qt_ref[0, h], preferred_element_type=F32)
            m_prev = m_ref[h]
            m_new = jnp.maximum(m_prev, jnp.max(s_t, axis=0, keepdims=True))
            alpha = jnp.exp2(m_prev - m_new)
            p_t = jnp.exp2(s_t - m_new)
            l_ref[h] = alpha * l_ref[h] + key_partial_sums(p_t)
            acc_ref[h] = alpha * acc_ref[h] + jnp.dot(vt, p_t.astype(BF16), preferred_element_type=F32)
            m_ref[h] = m_new

    @pl.when(j == pl.num_programs(3) - 1)
    def _():
        for h in range(GQA_GROUP):
            l = jnp.sum(l_ref[h], axis=0, keepdims=True)
            ot_ref[0, HEAD_DIM * h:HEAD_DIM * (h + 1), :] = (acc_ref[h] * (1.0 / l)).astype(BF16)


def _attn_a(bounded, qt, k, vt):
    b, _, _, s = qt.shape
    tq, tk = ATT_TQ, ATT_TK
    gw = GQA_GROUP * HEAD_DIM
    return pl.pallas_call(
        _attn_a_kernel,
        grid=(b, N_KV_A, s // tq, s // tk),
        in_specs=[
            pl.BlockSpec(memory_space=pltpu.SMEM),
            pl.BlockSpec((1, GQA_GROUP, 2 * HEAD_DIM, tq), lambda bi, g, i, j: (bi, g, 0, i)),
            pl.BlockSpec((1, tk, KV_WIDTH_A), lambda bi, g, i, j: (bi, j, 0)),
            pl.BlockSpec((1, 1, HEAD_DIM, tk), lambda bi, g, i, j: (bi, g, 0, j)),
        ],
        out_specs=pl.BlockSpec((1, gw, tq), lambda bi, g, i, j: (bi, g, i)),
        out_shape=jax.ShapeDtypeStruct((b, WIDTH_A, s), BF16),
        scratch_shapes=[
            pltpu.VMEM((GQA_GROUP, 1, tq), F32),
            pltpu.VMEM((GQA_GROUP, SUBLANES, tq), F32),
            pltpu.VMEM((GQA_GROUP, HEAD_DIM, tq), F32),
        ],
        compiler_params=pltpu.CompilerParams(
            dimension_semantics=("parallel", "parallel", "parallel", "arbitrary"),
            vmem_limit_bytes=VMEM_LIMIT_BYTES),
        name="attn_a",
    )(bounded, qt, k, vt)


def _dil_bias_init(bias_ref, dilation):
    tk = DIL_SUB + 2 * DIL_BLOCK
    qi = lax.broadcasted_iota(jnp.int32, (DIL_SUB, tk), 0)
    kj = lax.broadcasted_iota(jnp.int32, (DIL_SUB, tk), 1)
    aoff = jnp.abs(kj - DIL_BLOCK - qi)
    dist = (aoff * dilation).astype(F32)
    for h in range(N_HEADS_B):
        slope = 2.0 ** (-8.0 * (h + 1) / N_HEADS_B) * math.log2(math.e)
        bias_ref[h] = jnp.where(aoff <= DIL_BLOCK, -slope * dist, NEG_INF)


def _dil_sub_tiles(bounded, q_ref, kext_ref, vext_ref, o_ref, lse_ref, bias_ref, *, n_sub, first_key, length):
    tk = DIL_SUB + 2 * DIL_BLOCK
    lane = lax.broadcasted_iota(jnp.int32, (DIL_SUB, LANES), 1)
    first_head = lane < HEAD_DIM
    lse_owner = lane // LSE_LANES
    key_lane = lax.broadcasted_iota(jnp.int32, (1, tk), 1)

    def sub_tile(st, carry, *, use_max):
        r0 = pl.multiple_of(st * DIL_SUB, DIL_SUB)
        kidx = first_key + r0 + key_lane
        in_range = (kidx >= 0) & (kidx < length)
        lse_tile = jnp.zeros((DIL_SUB, LANES), F32)
        for hp in range(N_HEADS_B // 2):
            cols = slice(LANES * hp, LANES * (hp + 1))
            qp = q_ref[pl.ds(r0, DIL_SUB), cols]
            kp = kext_ref[pl.ds(r0, tk), cols]
            vp = vext_ref[pl.ds(r0, tk), cols]
            outs = []
            for e in range(2):
                h = 2 * hp + e
                own = first_head if e == 0 else jnp.logical_not(first_head)
                qm = jnp.where(own, qp, jnp.zeros_like(qp))
                s = _nt_dot(qm, kp) + bias_ref[h]
                s = jnp.where(in_range, s, NEG_INF)
                if use_max:
                    m = jnp.max(s, axis=-1, keepdims=True)
                    p = jnp.exp2(s - m)
                else:
                    p = jnp.exp2(s)
                l = jnp.sum(p, axis=-1, keepdims=True)
                pv = jnp.dot(p.astype(BF16), vp, preferred_element_type=F32)
                outs.append(pv * (1.0 / l))
                lse = jnp.log(l) + m * math.log(2.0) if use_max else jnp.log(l)
                lse_tile = jnp.where(lse_owner == h, lse, lse_tile)
            o_ref[pl.ds(r0, DIL_SUB), cols] = jnp.where(first_head, outs[0], outs[1]).astype(BF16)
        lse_ref[pl.ds(r0, DIL_SUB), :] = lse_tile
        return carry

    @pl.when(bounded)
    def _():
        lax.fori_loop(0, n_sub, functools.partial(sub_tile, use_max=False), 0, unroll=DIL_UNROLL)

    @pl.when(jnp.logical_not(bounded))
    def _():
        lax.fori_loop(0, n_sub, functools.partial(sub_tile, use_max=True), 0)


def _dil_rows_kernel(bounded_ref, q_ref, kp_ref, kc_ref, kn_ref, vp_ref, vc_ref, vn_ref, o_ref, lse_ref,
                     kext_ref, vext_ref, bias_ref, *, length):
    tl = q_ref.shape[1]
    bi, i = pl.program_id(0), pl.program_id(1)

    @pl.when((bi == 0) & (i == 0))
    def _():
        _dil_bias_init(bias_ref, 1)

    kext_ref[0:DIL_BLOCK] = kp_ref[0]
    kext_ref[DIL_BLOCK:DIL_BLOCK + tl] = kc_ref[0]
    kext_ref[DIL_BLOCK + tl:] = kn_ref[0]
    vext_ref[0:DIL_BLOCK] = vp_ref[0]
    vext_ref[DIL_BLOCK:DIL_BLOCK + tl] = vc_ref[0]
    vext_ref[DIL_BLOCK + tl:] = vn_ref[0]
    _dil_sub_tiles(bounded_ref[0] != 0, q_ref.at[0], kext_ref, vext_ref, o_ref.at[0], lse_ref.at[0], bias_ref,
                   n_sub=tl // DIL_SUB, first_key=i * tl - DIL_BLOCK, length=length)


def _dilated_rows(bounded, q, k, v):
    b, s, w = q.shape
    tl = DIL_TL
    blocks_per_tile = tl // DIL_BLOCK
    n_blocks = s // DIL_BLOCK
    cur = pl.BlockSpec((1, tl, w), lambda bi, i: (bi, i, 0))
    prev = pl.BlockSpec((1, DIL_BLOCK, w), lambda bi, i: (bi, jnp.maximum(i * blocks_per_tile - 1, 0), 0))
    nxt = pl.BlockSpec((1, DIL_BLOCK, w),
                       lambda bi, i: (bi, jnp.minimum((i + 1) * blocks_per_tile, n_blocks - 1), 0))
    return pl.pallas_call(
        functools.partial(_dil_rows_kernel, length=s),
        grid=(b, s // tl),
        in_specs=[pl.BlockSpec(memory_space=pltpu.SMEM), cur, prev, cur, nxt, prev, cur, nxt],
        out_specs=(cur, pl.BlockSpec((1, tl, LANES), lambda bi, i: (bi, i, 0))),
        out_shape=(jax.ShapeDtypeStruct((b, s, w), BF16), jax.ShapeDtypeStruct((b, s, LANES), F32)),
        scratch_shapes=[
            pltpu.VMEM((tl + 2 * DIL_BLOCK, w), BF16),
            pltpu.VMEM((tl + 2 * DIL_BLOCK, w), BF16),
            pltpu.VMEM((N_HEADS_B, DIL_SUB, DIL_SUB + 2 * DIL_BLOCK), F32),
        ],
        compiler_params=pltpu.CompilerParams(
            dimension_semantics=("arbitrary", "arbitrary"), vmem_limit_bytes=VMEM_LIMIT_BYTES),
        name="dil_1",
    )(bounded, q, k, k, k, v, v, v)


def _dil_grouped_kernel(bounded_ref, q_ref, k_ref, v_ref, o_ref, lse_ref,
                        qflat_ref, kext_ref, vext_ref, oflat_ref, lseflat_ref, bias_ref, *, dilation):
    n_tiles, rows = q_ref.shape[1], q_ref.shape[2]
    length = n_tiles * rows
    bi, r = pl.program_id(0), pl.program_id(1)

    @pl.when((bi == 0) & (r == 0))
    def _():
        _dil_bias_init(bias_ref, dilation)

    border = jnp.zeros((DIL_BLOCK, kext_ref.shape[1]), BF16)
    for ext_ref in (kext_ref, vext_ref):
        ext_ref[0:DIL_BLOCK] = border
        ext_ref[DIL_BLOCK + length:] = border
    for t in range(n_tiles):
        qflat_ref[rows * t:rows * (t + 1)] = q_ref[0, t]
        kext_ref[DIL_BLOCK + rows * t:DIL_BLOCK + rows * (t + 1)] = k_ref[0, t]
        vext_ref[DIL_BLOCK + rows * t:DIL_BLOCK + rows * (t + 1)] = v_ref[0, t]
    _dil_sub_tiles(bounded_ref[0] != 0, qflat_ref, kext_ref, vext_ref, oflat_ref, lseflat_ref, bias_ref,
                   n_sub=length // DIL_SUB, first_key=-DIL_BLOCK, length=length)
    for t in range(n_tiles):
        o_ref[0, t] = oflat_ref[rows * t:rows * (t + 1)]
        lse_ref[0, t] = lseflat_ref[rows * t:rows * (t + 1)]


def _dilated_grouped(bounded, q, k, v, dilation):
    b, s, w = q.shape
    n_tiles, rows = s // PROJ_TM, PROJ_TM // dilation
    length = n_tiles * rows
    view = lambda a: a.reshape(b, n_tiles, dilation, rows, a.shape[-1])
    spec = lambda width: pl.BlockSpec((1, n_tiles, None, rows, width), lambda bi, r: (bi, 0, r, 0, 0))
    o, lse = pl.pallas_call(
        functools.partial(_dil_grouped_kernel, dilation=dilation),
        grid=(b, dilation),
        in_specs=[pl.BlockSpec(memory_space=pltpu.SMEM), spec(w), spec(w), spec(w)],
        out_specs=(spec(w), spec(LANES)),
        out_shape=(jax.ShapeDtypeStruct((b, n_tiles, dilation, rows, w), BF16),
                   jax.ShapeDtypeStruct((b, n_tiles, dilation, rows, LANES), F32)),
        scratch_shapes=[
            pltpu.VMEM((length, w), BF16),
            pltpu.VMEM((length + 2 * DIL_BLOCK, w), BF16),
            pltpu.VMEM((length + 2 * DIL_BLOCK, w), BF16),
            pltpu.VMEM((length, w), BF16),
            pltpu.VMEM((length, LANES), F32),
            pltpu.VMEM((N_HEADS_B, DIL_SUB, DIL_SUB + 2 * DIL_BLOCK), F32),
        ],
        compiler_params=pltpu.CompilerParams(
            dimension_semantics=("arbitrary", "arbitrary"), vmem_limit_bytes=VMEM_LIMIT_BYTES),
        name=f"dil_{dilation}",
    )(bounded, view(q), view(k), view(v))
    return o.reshape(b, s, w), lse.reshape(b, s, LANES)


def _merge_kernel(x_ref, oat_ref, o1_ref, o2_ref, o3_ref, l1_ref, l2_ref, l3_ref,
                  ga_ref, gb_ref, wa_ref, wb_ref, unperm_ref, out_ref):
    def group_norm(y, g_ref):
        ms = jnp.mean(y * y, axis=-1, keepdims=True)
        return (y * lax.rsqrt(ms + NORM_EPS) * g_ref[...]).astype(BF16)

    oa = oat_ref[0].astype(F32).T

    def ungroup(n, val):
        return jnp.dot(unperm_ref[n], val, preferred_element_type=F32)

    def limbs(val):
        hi = val.astype(BF16)
        return hi, (val - hi.astype(F32)).astype(BF16)

    def ungroup_f32(n, val):
        both = ungroup(n, jnp.concatenate(limbs(val), axis=1))
        return both[:, :LANES] + both[:, LANES:]

    l1, l2, l3 = l1_ref[0], ungroup_f32(0, l2_ref[0]), ungroup_f32(1, l3_ref[0])
    mx = jnp.maximum(jnp.maximum(l1, l2), l3)
    e1, e2, e3 = jnp.exp(l1 - mx), jnp.exp(l2 - mx), jnp.exp(l3 - mx)
    inv = 1.0 / (e1 + e2 + e3)
    ri = lax.broadcasted_iota(jnp.int32, (2 * LANES, WIDTH_B), 0) % LANES
    ci = lax.broadcasted_iota(jnp.int32, (2 * LANES, WIDTH_B), 1)
    expand = jnp.where(ri == (ci // HEAD_DIM) * LSE_LANES, 1.0, 0.0).astype(BF16)

    def widen(wgt):
        return jnp.dot(jnp.concatenate(limbs(wgt), axis=1), expand, preferred_element_type=F32)

    ob = (widen(e1 * inv) * o1_ref[0].astype(F32)
          + widen(e2 * inv) * ungroup(0, o2_ref[0])
          + widen(e3 * inv) * ungroup(1, o3_ref[0]))
    mixed_a = group_norm(oa, ga_ref)
    mixed_b = group_norm(ob, gb_ref)
    out_ref[0] = (x_ref[0]
                  + jnp.dot(mixed_a, wa_ref[...], preferred_element_type=F32)
                  + jnp.dot(mixed_b, wb_ref[...], preferred_element_type=F32))


def _merge(x, oat, obs, lses, ga, gb, wo_a, wo_b, grouped_dilations):
    b, s, d = x.shape
    tm = PROJ_TM
    tau = jnp.arange(tm)
    unperm = jnp.stack([(jnp.arange(tm)[None, :] == _group_index(tau, tm, dil)[:, None]).astype(BF16)
                        for dil in grouped_dilations])
    full = lambda shape: pl.BlockSpec(shape, lambda bi, i: (0,) * len(shape))
    row = lambda w: pl.BlockSpec((1, tm, w), lambda bi, i: (bi, i, 0))
    return pl.pallas_call(
        _merge_kernel,
        grid=(b, s // tm),
        in_specs=[row(d), pl.BlockSpec((1, WIDTH_A, tm), lambda bi, i: (bi, 0, i)),
                  row(WIDTH_B), row(WIDTH_B), row(WIDTH_B), row(LANES), row(LANES), row(LANES),
                  full(ga.shape), full(gb.shape), full(wo_a.shape), full(wo_b.shape), full(unperm.shape)],
        out_specs=row(d),
        out_shape=jax.ShapeDtypeStruct((b, s, d), F32),
        compiler_params=pltpu.CompilerParams(
            dimension_semantics=("parallel", "parallel"), vmem_limit_bytes=VMEM_LIMIT_BYTES),
        name="merge",
    )(x, oat, *obs, *lses, ga, gb, wo_a, wo_b, unperm)


def _gelu_tanh(x):
    c = math.sqrt(2.0 / math.pi)
    return 0.5 * x * (1.0 + jnp.tanh(c * (x + 0.044715 * (x * x * x))))


def _ffn_kernel(xp_ref, xc_ref, xn_ref, g2_ref, wup_ref, cw_ref, cb_ref, wdn_ref, out_ref,
                hext_ref, act_ref, *, d_ff):
    tm = xc_ref.shape[1]
    halo = xp_ref.shape[1]
    i = pl.program_id(1)
    rows = tm + 2 * halo

    def normed(x):
        ms = jnp.mean(x * x, axis=-1, keepdims=True)
        return x * lax.rsqrt(ms + NORM_EPS) * g2_ref[...]

    keep_prev = (i > 0).astype(F32)
    keep_next = (i < pl.num_programs(1) - 1).astype(F32)
    hext_ref[0:halo] = (normed(xp_ref[0]) * keep_prev).astype(BF16)
    hext_ref[halo:halo + tm] = normed(xc_ref[0]).astype(BF16)
    hext_ref[halo + tm:] = (normed(xn_ref[0]) * keep_next).astype(BF16)
    hext = hext_ref[...]

    def conv(u, col):
        w = cw_ref[:, col:col + FFN_FC]
        up = pltpu.roll(u, 1, 0)
        dn = pltpu.roll(u, rows - 1, 0)
        y = up * w[0:1] + u * w[1:2] + dn * w[2:3] + cb_ref[:, col:col + FFN_FC]
        return y[halo:halo + tm]

    for c in range(0, d_ff, FFN_FC):
        ug = jnp.dot(hext, wup_ref[:, c:c + FFN_FC], preferred_element_type=F32)
        uv = jnp.dot(hext, wup_ref[:, d_ff + c:d_ff + c + FFN_FC], preferred_element_type=F32)
        act_ref[:, c:c + FFN_FC] = (_gelu_tanh(conv(ug, c)) * conv(uv, d_ff + c)).astype(BF16)

    out_ref[0] = xc_ref[0] + jnp.dot(act_ref[...], wdn_ref[...], preferred_element_type=F32)


def _ffn(x, g2, w_up, conv_w, conv_b, w_down):
    b, s, d = x.shape
    d_ff = w_down.shape[0]
    tm, halo = FFN_TM, FFN_HALO
    per_tile = tm // halo
    n_halo_blocks = s // halo
    full = lambda shape: pl.BlockSpec(shape, lambda bi, i: (0,) * len(shape))
    resident = lambda shape: pl.BlockSpec(shape, lambda bi, i: (0,) * len(shape), pipeline_mode=pl.Buffered(1))
    return pl.pallas_call(
        functools.partial(_ffn_kernel, d_ff=d_ff),
        grid=(b, s // tm),
        in_specs=[
            pl.BlockSpec((1, halo, d), lambda bi, i: (bi, jnp.maximum(i * per_tile - 1, 0), 0)),
            pl.BlockSpec((1, tm, d), lambda bi, i: (bi, i, 0)),
            pl.BlockSpec((1, halo, d), lambda bi, i: (bi, jnp.minimum((i + 1) * per_tile, n_halo_blocks - 1), 0)),
            full(g2.shape), resident(w_up.shape), full(conv_w.shape), full(conv_b.shape), resident(w_down.shape),
        ],
        out_specs=pl.BlockSpec((1, tm, d), lambda bi, i: (bi, i, 0)),
        out_shape=jax.ShapeDtypeStruct((b, s, d), F32),
        scratch_shapes=[pltpu.VMEM((tm + 2 * halo, d), BF16), pltpu.VMEM((tm, d_ff), BF16)],
        compiler_params=pltpu.CompilerParams(
            dimension_semantics=("parallel", "parallel"), vmem_limit_bytes=VMEM_LIMIT_BYTES),
        name="ffn",
    )(x, x, x, g2, w_up, conv_w, conv_b, w_down)


def _rope_tables_t(seq_len):
    rows = seq_len // GRID_W
    row = jnp.repeat(jnp.arange(rows, dtype=F32), GRID_W)
    col = jnp.tile(jnp.arange(GRID_W, dtype=F32), rows)
    inv = ROPE_THETA ** (-jnp.arange(0, ROPE_AXIS_DIM, 2, dtype=F32) / ROPE_AXIS_DIM)
    ang_r = (row[:, None] * inv[None, :]).T
    ang_c = (col[:, None] * inv[None, :]).T
    cos_t = jnp.concatenate([jnp.cos(ang_r)] * 2 + [jnp.cos(ang_c)] * 2, axis=0)
    sin_t = jnp.concatenate([-jnp.sin(ang_r), jnp.sin(ang_r), -jnp.sin(ang_c), jnp.sin(ang_c)], axis=0)
    return cos_t, sin_t


def kernel(x, norm1_g, w_in, qa_norm_g, ka_norm_g, qb_norm_g, kb_norm_g, outa_norm_g, outb_norm_g,
           w_out, norm2_g, w_up, conv_w, conv_b, w_down):
    b, s, d = x.shape
    scale = HEAD_DIM ** -0.5 * math.log2(math.e)
    a_cols = WIDTH_A + 2 * KV_WIDTH_A
    wa = w_in[:, :a_cols].astype(BF16)
    wb = w_in[:, a_cols:].astype(BF16)
    ga = jnp.concatenate([jnp.tile(qa_norm_g, N_HEADS_A) * scale, jnp.tile(ka_norm_g, N_KV_A)])
    ga = jnp.broadcast_to(ga[:, None], (ga.shape[0], LANES))
    gqb = (jnp.tile(qb_norm_g, N_HEADS_B) * scale)[None, :]
    gkb = jnp.tile(kb_norm_g, N_HEADS_B)[None, :]
    cos_t, sin_t = _rope_tables_t(s)

    dilations = [dil for _, dil in DILATED_PATTERNS]
    assert dilations[0] == 1
    grouped_dilations = dilations[1:]
    qt, k, vt, *qkv_b = _proj(x, norm1_g[None, :], wa, wb, ga, cos_t, sin_t, gqb, gkb, grouped_dilations)

    def scores_bounded(gq, gk):
        bound = math.sqrt(HEAD_DIM) * jnp.max(jnp.abs(gq)) * jnp.max(jnp.abs(gk))
        return (bound <= SAFE_SCORE_BOUND).astype(jnp.int32).reshape(1)

    oat = _attn_a(scores_bounded(qa_norm_g, ka_norm_g), qt, k, vt)
    bounded_b = scores_bounded(qb_norm_g, kb_norm_g)
    results = [_dilated_rows(bounded_b, *qkv_b[:3])]
    for n, dil in enumerate(grouped_dilations):
        results.append(_dilated_grouped(bounded_b, *qkv_b[3 * (n + 1):3 * (n + 2)], dil))
    obs, lses = zip(*results)
    wo = w_out.astype(BF16)
    x2 = _merge(x, oat, obs, lses, outa_norm_g[None, :], outb_norm_g[None, :], wo[:WIDTH_A], wo[WIDTH_A:],
                grouped_dilations)
    return _ffn(x2, norm2_g[None, :], w_up.astype(BF16), conv_w, conv_b[None, :], w_down.astype(BF16))
```

```python
import functools
import math

import jax
import jax.numpy as jnp
from jax import lax
from jax.experimental import pallas as pl
from jax.experimental.pallas import tpu as pltpu

F32 = jnp.float32
BF16 = jnp.bfloat16

HEAD_DIM = 64
N_HEADS_A = 8
N_KV_A = 2
GQA_GROUP = N_HEADS_A // N_KV_A
N_HEADS_B = 8
WIDTH_A = N_HEADS_A * HEAD_DIM
WIDTH_B = N_HEADS_B * HEAD_DIM
KV_WIDTH_A = N_KV_A * HEAD_DIM
GRID_W = 64
ROPE_THETA = 10000.0
ROPE_AXIS_DIM = HEAD_DIM // 2
DIL_BLOCK = 64
DILATED_PATTERNS = ((128, 1), (512, 4), (2048, 16))
NORM_EPS = 1e-6
NEG_INF = -1e30

LANES = 128
SUBLANES = 8
MXU_DIM = 256
VMEM_LIMIT_BYTES = 56 * 1024 * 1024

PROJ_TM = 1024
GROUP_TILE = 512
GROUP_STEP = 4
ATT_TQ = 2048
ATT_TK = 1024
SAFE_SCORE_BOUND = 40.0
DIL_TL = 512
DIL_SUB = 128
DIL_UNROLL = 4
FFN_TM = 1024
FFN_FC = 256
FFN_HALO = 16
LSE_LANES = 16


def _nt_dot(a, b):
    return lax.dot_general(a, b, (((1,), (1,)), ((), ())), preferred_element_type=F32)


def _proj_kernel(x_ref, g1_ref, wa_ref, wb_ref, ga_ref, cos_ref, sin_ref, gqb_ref, gkb_ref,
                 qt_ref, k_ref, vt_ref, qb_ref, kb_ref, vb_ref, *rest):
    *grouped_refs, nat_all_ref, lvl_all_ref = rest
    for part in range(x_ref.shape[1] // GROUP_TILE):
        _proj_rows(part * GROUP_TILE, x_ref, g1_ref, wa_ref, wb_ref, ga_ref, cos_ref, sin_ref, gqb_ref, gkb_ref,
                   qt_ref, k_ref, vt_ref, qb_ref, kb_ref, vb_ref, grouped_refs,
                   nat_all_ref.at[part], lvl_all_ref.at[part])


def _proj_rows(r0, x_ref, g1_ref, wa_ref, wb_ref, ga_ref, cos_ref, sin_ref, gqb_ref, gkb_ref,
               qt_ref, k_ref, vt_ref, qb_ref, kb_ref, vb_ref, grouped_refs, nat_ref, lvl_ref):
    tm = GROUP_TILE
    rows = slice(r0, r0 + tm)
    x = x_ref[0, rows]
    ms = jnp.mean(x * x, axis=-1, keepdims=True)
    hn = (x * lax.rsqrt(ms + NORM_EPS) * g1_ref[...]).astype(BF16)
    ya = jnp.dot(hn, wa_ref[...], preferred_element_type=F32)
    yb = jnp.dot(hn, wb_ref[...], preferred_element_type=F32)

    ya_t = ya.T
    for g in range(N_KV_A):
        v_rows = ya_t[WIDTH_A + KV_WIDTH_A + HEAD_DIM * g:WIDTH_A + KV_WIDTH_A + HEAD_DIM * (g + 1)]
        vt_ref[0, g, :, rows] = v_rows.astype(BF16)
    cos_t = cos_ref[:, rows]
    sin_t = sin_ref[:, rows]
    reps = tm // LANES
    zeros = jnp.zeros((HEAD_DIM, tm), BF16)
    k_parts = []
    q16 = ROPE_AXIS_DIM // 2
    for h in range(N_HEADS_A + N_KV_A):
        u = ya_t[HEAD_DIM * h:HEAD_DIM * (h + 1)]
        r = lax.rsqrt(jnp.mean(u * u, axis=0, keepdims=True) + NORM_EPS)
        g = jnp.tile(ga_ref[HEAD_DIM * h:HEAD_DIM * (h + 1), :], (1, reps))
        un = u * r * g
        partner = jnp.concatenate(
            [un[q16:2 * q16], un[0:q16], un[3 * q16:4 * q16], un[2 * q16:3 * q16]], axis=0)
        o = un * cos_t + partner * sin_t
        if h < N_HEADS_A:
            ob = o.astype(BF16)
            pieces = [ob, zeros] if h // GQA_GROUP == 0 else [zeros, ob]
            qt_ref[0, h, :, rows] = jnp.concatenate(pieces, axis=0)
        else:
            k_parts.append(o)
    k_ref[0, rows] = jnp.concatenate(k_parts, axis=0).T.astype(BF16)

    ri = lax.broadcasted_iota(jnp.int32, (MXU_DIM, MXU_DIM), 0) // HEAD_DIM
    ci = lax.broadcasted_iota(jnp.int32, (MXU_DIM, MXU_DIM), 1) // HEAD_DIM
    ones_bd = jnp.where(ri == ci, 1.0, 0.0).astype(BF16)

    def head_norm(y, g_ref):
        sq = (y * y).astype(BF16)
        ss = jnp.concatenate(
            [jnp.dot(sq[:, c:c + MXU_DIM], ones_bd, preferred_element_type=F32)
             for c in range(0, y.shape[1], MXU_DIM)], axis=1)
        return y * lax.rsqrt(ss * (1.0 / HEAD_DIM) + NORM_EPS) * g_ref[...]

    parts = (head_norm(yb[:, :WIDTH_B], gqb_ref), head_norm(yb[:, WIDTH_B:2 * WIDTH_B], gkb_ref),
             yb[:, 2 * WIDTH_B:])
    slabs = WIDTH_B // LANES
    for c, (part, ref) in enumerate(zip(parts, (qb_ref, kb_ref, vb_ref))):
        ref[0, rows] = part.astype(BF16)
        for jj in range(slabs):
            nat_ref[slabs * c + jj] = part[:, LANES * jj:LANES * (jj + 1)]

    src_ref, block = nat_ref, tm
    n_levels = len(grouped_refs) // 3
    for n in range(n_levels):
        sub = block // GROUP_STEP
        for j in range(3 * slabs):
            c, jj = divmod(j, slabs)
            dst_ref = grouped_refs[3 * n + c]
            for lo in range(0, tm, sub):
                blk0, r = (lo // block) * block, (lo % block) // sub
                picked = src_ref[j, pl.ds(blk0 + r, sub, stride=GROUP_STEP), :]
                if n + 1 < n_levels:
                    lvl_ref[j, lo:lo + sub, :] = picked
                dst_ref[0, r0 + lo:r0 + lo + sub, LANES * jj:LANES * (jj + 1)] = picked.astype(BF16)
        src_ref, block = lvl_ref, sub


def _group_index(tau, tm, dilation):
    base, size, local = 0, tm, tau
    while dilation > 1:
        size //= GROUP_STEP
        base = base + (local % GROUP_STEP) * size
        local = local // GROUP_STEP
        dilation //= GROUP_STEP
    return base + local


def _proj(x, g1, wa, wb, ga, cos_t, sin_t, gqb, gkb, grouped_dilations):
    b, s, d = x.shape
    tm = PROJ_TM
    assert tuple(grouped_dilations) == tuple(GROUP_STEP ** (n + 1) for n in range(len(grouped_dilations)))
    assert len(grouped_dilations) <= 2
    full = lambda shape: pl.BlockSpec(shape, lambda bi, i: (0,) * len(shape))
    rows_b = jax.ShapeDtypeStruct((b, s, WIDTH_B), BF16)
    out_shape = (
        jax.ShapeDtypeStruct((b, N_HEADS_A, 2 * HEAD_DIM, s), BF16),
        jax.ShapeDtypeStruct((b, s, KV_WIDTH_A), BF16),
        jax.ShapeDtypeStruct((b, N_KV_A, HEAD_DIM, s), BF16),
    ) + (rows_b,) * (3 + 3 * len(grouped_dilations))
    row = lambda w: pl.BlockSpec((1, tm, w), lambda bi, i: (bi, i, 0))
    return pl.pallas_call(
        _proj_kernel,
        grid=(b, s // tm),
        in_specs=[
            row(d), full(g1.shape), full(wa.shape), full(wb.shape), full(ga.shape),
            pl.BlockSpec((HEAD_DIM, tm), lambda bi, i: (0, i)),
            pl.BlockSpec((HEAD_DIM, tm), lambda bi, i: (0, i)),
            full(gqb.shape), full(gkb.shape),
        ],
        out_specs=(
            pl.BlockSpec((1, N_HEADS_A, 2 * HEAD_DIM, tm), lambda bi, i: (bi, 0, 0, i)),
            row(KV_WIDTH_A),
            pl.BlockSpec((1, N_KV_A, HEAD_DIM, tm), lambda bi, i: (bi, 0, 0, i)),
        ) + (row(WIDTH_B),) * (3 + 3 * len(grouped_dilations)),
        out_shape=out_shape,
        scratch_shapes=[pltpu.VMEM((tm // GROUP_TILE, 3 * WIDTH_B // LANES, GROUP_TILE, LANES), F32)] * 2,
        compiler_params=pltpu.CompilerParams(
            dimension_semantics=("parallel", "parallel"), vmem_limit_bytes=VMEM_LIMIT_BYTES),
        name="proj",
    )(x, g1, wa, wb, ga, cos_t, sin_t, gqb, gkb)


def _attn_a_kernel(bounded_ref, qt_ref, k_ref, vt_ref, ot_ref, m_ref, l_ref, acc_ref):
    j = pl.program_id(3)
    bounded = bounded_ref[0] != 0
    tk, tq = k_ref.shape[1], qt_ref.shape[3]

    @pl.when(j == 0)
    def _():
        m_ref[...] = jnp.full(m_ref.shape, NEG_INF, F32)
        l_ref[...] = jnp.zeros(l_ref.shape, F32)
        acc_ref[...] = jnp.zeros(acc_ref.shape, F32)

    k = k_ref[0]
    vt = vt_ref[0, 0]

    def key_partial_sums(p_t):
        return jnp.sum(p_t.reshape(tk // SUBLANES, SUBLANES, tq), axis=0)

    @pl.when(bounded)
    def _():
        for h in range(GQA_GROUP):
            s_t = jnp.dot(k, qt_ref[0, h], preferred_element_type=F32)
            p_t = jnp.exp2(s_t)
            l_ref[h] += key_partial_sums(p_t)
            acc_ref[h] += jnp.dot(vt, p_t.astype(BF16), preferred_element_type=F32)

    @pl.when(jnp.logical_not(bounded))
    def _():
        for h in range(GQA_GROUP):
            s_t = jnp.dot(k, qt_ref[0, h], preferred_element_type=F32)
            m_prev = m_ref[h]
            m_new = jnp.maximum(m_prev, jnp.max(s_t, axis=0, keepdims=True))
            alpha = jnp.exp2(m_prev - m_new)
            p_t = jnp.exp2(s_t - m_new)
            l_ref[h] = alpha * l_ref[h] + key_partial_sums(p_t)
            acc_ref[h] = alpha * acc_ref[h] + jnp.dot(vt, p_t.astype(BF16), preferred_element_type=F32)
            m_ref[h] = m_new

    @pl.when(j == pl.num_programs(3) - 1)
    def _():
        for h in range(GQA_GROUP):
            l = jnp.sum(l_ref[h], axis=0, keepdims=True)
            ot_ref[0, HEAD_DIM * h:HEAD_DIM * (h + 1), :] = (acc_ref[h] * (1.0 / l)).astype(BF16)


def _attn_a(bounded, qt, k, vt):
    b, _, _, s = qt.shape
    tq, tk = ATT_TQ, ATT_TK
    gw = GQA_GROUP * HEAD_DIM
    return pl.pallas_call(
        _attn_a_kernel,
        grid=(b, N_KV_A, s // tq, s // tk),
        in_specs=[
            pl.BlockSpec(memory_space=pltpu.SMEM),
            pl.BlockSpec((1, GQA_GROUP, 2 * HEAD_DIM, tq), lambda bi, g, i, j: (bi, g, 0, i)),
            pl.BlockSpec((1, tk, KV_WIDTH_A), lambda bi, g, i, j: (bi, j, 0)),
            pl.BlockSpec((1, 1, HEAD_DIM, tk), lambda bi, g, i, j: (bi, g, 0, j)),
        ],
        out_specs=pl.BlockSpec((1, gw, tq), lambda bi, g, i, j: (bi, g, i)),
        out_shape=jax.ShapeDtypeStruct((b, WIDTH_A, s), BF16),
        scratch_shapes=[
            pltpu.VMEM((GQA_GROUP, 1, tq), F32),
            pltpu.VMEM((GQA_GROUP, SUBLANES, tq), F32),
            pltpu.VMEM((GQA_GROUP, HEAD_DIM, tq), F32),
        ],
        compiler_params=pltpu.CompilerParams(
            dimension_semantics=("parallel", "parallel", "parallel", "arbitrary"),
            vmem_limit_bytes=VMEM_LIMIT_BYTES),
        name="attn_a",
    )(bounded, qt, k, vt)


def _dil_bias_init(bias_ref, dilation):
    tk = DIL_SUB + 2 * DIL_BLOCK
    qi = lax.broadcasted_iota(jnp.int32, (DIL_SUB, tk), 0)
    kj = lax.broadcasted_iota(jnp.int32, (DIL_SUB, tk), 1)
    aoff = jnp.abs(kj - DIL_BLOCK - qi)
    dist = (aoff * dilation).astype(F32)
    for h in range(N_HEADS_B):
        slope = 2.0 ** (-8.0 * (h + 1) / N_HEADS_B) * math.log2(math.e)
        bias_ref[h] = jnp.where(aoff <= DIL_BLOCK, -slope * dist, NEG_INF)


def _lse_lane_pair(lane):
    return (lane % HEAD_DIM) // LSE_LANES


def _lse_lane_of_head(h):
    return HEAD_DIM * (h % 2) + LSE_LANES * (h // 2)


def _dil_sub_tiles(bounded, q_ref, kext_ref, vext_ref, o_ref, lse_ref, bias_ref, *, n_sub, first_key, length):
    tk = DIL_SUB + 2 * DIL_BLOCK
    first_head = lax.broadcasted_iota(jnp.int32, (DIL_SUB, LANES), 1) < HEAD_DIM
    first_head_k = lax.broadcasted_iota(jnp.int32, (tk, LANES), 1) < HEAD_DIM
    lse_owner = _lse_lane_pair(lax.broadcasted_iota(jnp.int32, (DIL_SUB, LANES), 1))
    key_lane = lax.broadcasted_iota(jnp.int32, (1, tk), 1)
    stack_row = lax.broadcasted_iota(jnp.int32, (2 * tk, LANES), 0) < tk
    stack_lane = lax.broadcasted_iota(jnp.int32, (2 * tk, LANES), 1) < HEAD_DIM
    ones_stack = jnp.where(stack_row == stack_lane, 1.0, 0.0).astype(BF16)

    def sub_tile(st, carry, *, use_max):
        r0 = pl.multiple_of(st * DIL_SUB, DIL_SUB)
        kidx = first_key + r0 + key_lane
        in_range = (kidx >= 0) & (kidx < length)
        lse_tile = jnp.zeros((DIL_SUB, LANES), F32)
        for hp in range(N_HEADS_B // 2):
            cols = slice(LANES * hp, LANES * (hp + 1))
            qp = q_ref[pl.ds(r0, DIL_SUB), cols]
            kp = kext_ref[pl.ds(r0, tk), cols]
            vp = vext_ref[pl.ds(r0, tk), cols]
            zeros_v = jnp.zeros_like(vp)
            v_stack = jnp.concatenate([jnp.where(first_head_k, vp, zeros_v),
                                       jnp.where(first_head_k, zeros_v, vp)], axis=0)
            ps, ms = [], []
            for e in range(2):
                own = first_head if e == 0 else jnp.logical_not(first_head)
                qm = jnp.where(own, qp, jnp.zeros_like(qp))
                s = _nt_dot(qm, kp) + bias_ref[2 * hp + e]
                s = jnp.where(in_range, s, NEG_INF)
                if use_max:
                    ms.append(jnp.max(s, axis=-1, keepdims=True))
                    s = s - ms[-1]
                ps.append(jnp.exp2(s).astype(BF16))
            p_pair = jnp.concatenate(ps, axis=1)
            both = jnp.dot(p_pair, jnp.concatenate([v_stack, ones_stack], axis=1), preferred_element_type=F32)
            pv, l_pair = both[:, :LANES], both[:, LANES:]
            o_ref[pl.ds(r0, DIL_SUB), cols] = (pv * (1.0 / l_pair)).astype(BF16)
            lse = jnp.log(l_pair)
            if use_max:
                lse = lse + jnp.where(first_head, ms[0], ms[1]) * math.log(2.0)
            lse_tile = jnp.where(lse_owner == hp, lse, lse_tile)
        lse_ref[pl.ds(r0, DIL_SUB), :] = lse_tile
        return carry

    @pl.when(bounded)
    def _():
        lax.fori_loop(0, n_sub, functools.partial(sub_tile, use_max=False), 0, unroll=DIL_UNROLL)

    @pl.when(jnp.logical_not(bounded))
    def _():
        lax.fori_loop(0, n_sub, functools.partial(sub_tile, use_max=True), 0)


def _dil_rows_kernel(bounded_ref, q_ref, kp_ref, kc_ref, kn_ref, vp_ref, vc_ref, vn_ref, o_ref, lse_ref,
                     kext_ref, vext_ref, bias_ref, *, length):
    tl = q_ref.shape[1]
    bi, i = pl.program_id(0), pl.program_id(1)

    @pl.when((bi == 0) & (i == 0))
    def _():
        _dil_bias_init(bias_ref, 1)

    kext_ref[0:DIL_BLOCK] = kp_ref[0]
    kext_ref[DIL_BLOCK:DIL_BLOCK + tl] = kc_ref[0]
    kext_ref[DIL_BLOCK + tl:] = kn_ref[0]
    vext_ref[0:DIL_BLOCK] = vp_ref[0]
    vext_ref[DIL_BLOCK:DIL_BLOCK + tl] = vc_ref[0]
    vext_ref[DIL_BLOCK + tl:] = vn_ref[0]
    _dil_sub_tiles(bounded_ref[0] != 0, q_ref.at[0], kext_ref, vext_ref, o_ref.at[0], lse_ref.at[0], bias_ref,
                   n_sub=tl // DIL_SUB, first_key=i * tl - DIL_BLOCK, length=length)


def _dilated_rows(bounded, q, k, v):
    b, s, w = q.shape
    tl = DIL_TL
    blocks_per_tile = tl // DIL_BLOCK
    n_blocks = s // DIL_BLOCK
    cur = pl.BlockSpec((1, tl, w), lambda bi, i: (bi, i, 0))
    prev = pl.BlockSpec((1, DIL_BLOCK, w), lambda bi, i: (bi, jnp.maximum(i * blocks_per_tile - 1, 0), 0))
    nxt = pl.BlockSpec((1, DIL_BLOCK, w),
                       lambda bi, i: (bi, jnp.minimum((i + 1) * blocks_per_tile, n_blocks - 1), 0))
    return pl.pallas_call(
        functools.partial(_dil_rows_kernel, length=s),
        grid=(b, s // tl),
        in_specs=[pl.BlockSpec(memory_space=pltpu.SMEM), cur, prev, cur, nxt, prev, cur, nxt],
        out_specs=(cur, pl.BlockSpec((1, tl, LANES), lambda bi, i: (bi, i, 0))),
        out_shape=(jax.ShapeDtypeStruct((b, s, w), BF16), jax.ShapeDtypeStruct((b, s, LANES), F32)),
        scratch_shapes=[
            pltpu.VMEM((tl + 2 * DIL_BLOCK, w), BF16),
            pltpu.VMEM((tl + 2 * DIL_BLOCK, w), BF16),
            pltpu.VMEM((N_HEADS_B, DIL_SUB, DIL_SUB + 2 * DIL_BLOCK), F32),
        ],
        compiler_params=pltpu.CompilerParams(
            dimension_semantics=("arbitrary", "arbitrary"), vmem_limit_bytes=VMEM_LIMIT_BYTES),
        name="dil_1",
    )(bounded, q, k, k, k, v, v, v)


def _dil_grouped_kernel(bounded_ref, q_ref, k_ref, v_ref, o_ref, lse_ref,
                        qflat_ref, kext_ref, vext_ref, oflat_ref, lseflat_ref, bias_ref, *, dilation):
    n_tiles, rows = q_ref.shape[1], q_ref.shape[2]
    length = n_tiles * rows
    bi, r = pl.program_id(0), pl.program_id(1)

    @pl.when((bi == 0) & (r == 0))
    def _():
        _dil_bias_init(bias_ref, dilation)

    border = jnp.zeros((DIL_BLOCK, kext_ref.shape[1]), BF16)
    for ext_ref in (kext_ref, vext_ref):
        ext_ref[0:DIL_BLOCK] = border
        ext_ref[DIL_BLOCK + length:] = border
    for t in range(n_tiles):
        qflat_ref[rows * t:rows * (t + 1)] = q_ref[0, t]
        kext_ref[DIL_BLOCK + rows * t:DIL_BLOCK + rows * (t + 1)] = k_ref[0, t]
        vext_ref[DIL_BLOCK + rows * t:DIL_BLOCK + rows * (t + 1)] = v_ref[0, t]
    _dil_sub_tiles(bounded_ref[0] != 0, qflat_ref, kext_ref, vext_ref, oflat_ref, lseflat_ref, bias_ref,
                   n_sub=length // DIL_SUB, first_key=-DIL_BLOCK, length=length)
    for t in range(n_tiles):
        o_ref[0, t] = oflat_ref[rows * t:rows * (t + 1)]
        lse_ref[0, t] = lseflat_ref[rows * t:rows * (t + 1)]


def _dilated_grouped(bounded, q, k, v, dilation):
    b, s, w = q.shape
    n_tiles, rows = s // GROUP_TILE, GROUP_TILE // dilation
    length = n_tiles * rows
    view = lambda a: a.reshape(b, n_tiles, dilation, rows, a.shape[-1])
    spec = lambda width: pl.BlockSpec((1, n_tiles, None, rows, width), lambda bi, r: (bi, 0, r, 0, 0))
    o, lse = pl.pallas_call(
        functools.partial(_dil_grouped_kernel, dilation=dilation),
        grid=(b, dilation),
        in_specs=[pl.BlockSpec(memory_space=pltpu.SMEM), spec(w), spec(w), spec(w)],
        out_specs=(spec(w), spec(LANES)),
        out_shape=(jax.ShapeDtypeStruct((b, n_tiles, dilation, rows, w), BF16),
                   jax.ShapeDtypeStruct((b, n_tiles, dilation, rows, LANES), F32)),
        scratch_shapes=[
            pltpu.VMEM((length, w), BF16),
            pltpu.VMEM((length + 2 * DIL_BLOCK, w), BF16),
            pltpu.VMEM((length + 2 * DIL_BLOCK, w), BF16),
            pltpu.VMEM((length, w), BF16),
            pltpu.VMEM((length, LANES), F32),
            pltpu.VMEM((N_HEADS_B, DIL_SUB, DIL_SUB + 2 * DIL_BLOCK), F32),
        ],
        compiler_params=pltpu.CompilerParams(
            dimension_semantics=("arbitrary", "arbitrary"), vmem_limit_bytes=VMEM_LIMIT_BYTES),
        name=f"dil_{dilation}",
    )(bounded, view(q), view(k), view(v))
    return o.reshape(b, s, w), lse.reshape(b, s, LANES)


def _ungroup_rows(val, levels, scratch_refs):
    slabs = val.shape[1] // LANES
    for lvl in reversed(range(levels)):
        block = GROUP_TILE // GROUP_STEP ** lvl
        sub = block // GROUP_STEP
        ref = scratch_refs[lvl]
        for j in range(slabs):
            for lo in range(0, GROUP_TILE, sub):
                blk0, r = (lo // block) * block, (lo % block) // sub
                ref[j, pl.ds(blk0 + r, sub, stride=GROUP_STEP), :] = val[lo:lo + sub, LANES * j:LANES * (j + 1)]
        val = jnp.concatenate([ref[j] for j in range(slabs)], axis=1)
    return val


def _merge_kernel(x_ref, oat_ref, o1_ref, o2_ref, o3_ref, l1_ref, l2_ref, l3_ref,
                  ga_ref, gb_ref, wa_ref, wb_ref, out_ref, *scratch):
    def group_norm(y, g_ref):
        ms = jnp.mean(y * y, axis=-1, keepdims=True)
        return (y * lax.rsqrt(ms + NORM_EPS) * g_ref[...]).astype(BF16)

    oa = oat_ref[0].astype(F32).T

    o2_scr, o3_scr, l2_scr, l3_scr = scratch[0:1], scratch[1:3], scratch[3:4], scratch[4:6]

    def limbs(val):
        hi = val.astype(BF16)
        return hi, (val - hi.astype(F32)).astype(BF16)

    l1, l2, l3 = l1_ref[0], _ungroup_rows(l2_ref[0], 1, l2_scr), _ungroup_rows(l3_ref[0], 2, l3_scr)
    mx = jnp.maximum(jnp.maximum(l1, l2), l3)
    e1, e2, e3 = jnp.exp(l1 - mx), jnp.exp(l2 - mx), jnp.exp(l3 - mx)
    inv = 1.0 / (e1 + e2 + e3)
    ri = lax.broadcasted_iota(jnp.int32, (2 * LANES, WIDTH_B), 0) % LANES
    ci = lax.broadcasted_iota(jnp.int32, (2 * LANES, WIDTH_B), 1)
    expand = jnp.where(ri == _lse_lane_of_head(ci // HEAD_DIM), 1.0, 0.0).astype(BF16)

    def widen(wgt):
        return jnp.dot(jnp.concatenate(limbs(wgt), axis=1), expand, preferred_element_type=F32)

    ob = (widen(e1 * inv) * o1_ref[0].astype(F32)
          + widen(e2 * inv) * _ungroup_rows(o2_ref[0].astype(F32), 1, o2_scr)
          + widen(e3 * inv) * _ungroup_rows(o3_ref[0].astype(F32), 2, o3_scr))
    mixed_a = group_norm(oa, ga_ref)
    mixed_b = group_norm(ob, gb_ref)
    out_ref[0] = (x_ref[0]
                  + jnp.dot(mixed_a, wa_ref[...], preferred_element_type=F32)
                  + jnp.dot(mixed_b, wb_ref[...], preferred_element_type=F32))


def _merge(x, oat, obs, lses, ga, gb, wo_a, wo_b, grouped_dilations):
    b, s, d = x.shape
    tm = GROUP_TILE
    assert tuple(grouped_dilations) == (GROUP_STEP, GROUP_STEP ** 2)
    slab = lambda n: pltpu.VMEM((n, tm, LANES), F32)
    wide, narrow = WIDTH_B // LANES, 1
    full = lambda shape: pl.BlockSpec(shape, lambda bi, i: (0,) * len(shape))
    row = lambda w: pl.BlockSpec((1, tm, w), lambda bi, i: (bi, i, 0))
    return pl.pallas_call(
        _merge_kernel,
        grid=(b, s // tm),
        in_specs=[row(d), pl.BlockSpec((1, WIDTH_A, tm), lambda bi, i: (bi, 0, i)),
                  row(WIDTH_B), row(WIDTH_B), row(WIDTH_B), row(LANES), row(LANES), row(LANES),
                  full(ga.shape), full(gb.shape), full(wo_a.shape), full(wo_b.shape)],
        out_specs=row(d),
        out_shape=jax.ShapeDtypeStruct((b, s, d), F32),
        scratch_shapes=[slab(wide), slab(wide), slab(wide), slab(narrow), slab(narrow), slab(narrow)],
        compiler_params=pltpu.CompilerParams(
            dimension_semantics=("parallel", "parallel"), vmem_limit_bytes=VMEM_LIMIT_BYTES),
        name="merge",
    )(x, oat, *obs, *lses, ga, gb, wo_a, wo_b)


def _gelu_tanh(x):
    c = math.sqrt(2.0 / math.pi)
    return 0.5 * x * (1.0 + jnp.tanh(c * (x + 0.044715 * (x * x * x))))


def _ffn_kernel(xp_ref, xc_ref, xn_ref, g2_ref, wup_ref, cw_ref, cb_ref, wdn_ref, out_ref,
                hext_ref, act_ref, *, d_ff):
    tm = xc_ref.shape[1]
    halo = xp_ref.shape[1]
    i = pl.program_id(1)
    rows = tm + 2 * halo

    def normed(x):
        ms = jnp.mean(x * x, axis=-1, keepdims=True)
        return x * lax.rsqrt(ms + NORM_EPS) * g2_ref[...]

    keep_prev = (i > 0).astype(F32)
    keep_next = (i < pl.num_programs(1) - 1).astype(F32)
    hext_ref[0:halo] = (normed(xp_ref[0]) * keep_prev).astype(BF16)
    hext_ref[halo:halo + tm] = normed(xc_ref[0]).astype(BF16)
    hext_ref[halo + tm:] = (normed(xn_ref[0]) * keep_next).astype(BF16)
    hext = hext_ref[...]

    def conv(u, col):
        w = cw_ref[:, col:col + FFN_FC]
        up = pltpu.roll(u, 1, 0)
        dn = pltpu.roll(u, rows - 1, 0)
        y = up * w[0:1] + u * w[1:2] + dn * w[2:3] + cb_ref[:, col:col + FFN_FC]
        return y[halo:halo + tm]

    for c in range(0, d_ff, FFN_FC):
        ug = jnp.dot(hext, wup_ref[:, c:c + FFN_FC], preferred_element_type=F32)
        uv = jnp.dot(hext, wup_ref[:, d_ff + c:d_ff + c + FFN_FC], preferred_element_type=F32)
        act_ref[:, c:c + FFN_FC] = (_gelu_tanh(conv(ug, c)) * conv(uv, d_ff + c)).astype(BF16)

    out_ref[0] = xc_ref[0] + jnp.dot(act_ref[...], wdn_ref[...], preferred_element_type=F32)


def _ffn(x, g2, w_up, conv_w, conv_b, w_down):
    b, s, d = x.shape
    d_ff = w_down.shape[0]
    tm, halo = FFN_TM, FFN_HALO
    per_tile = tm // halo
    n_halo_blocks = s // halo
    full = lambda shape: pl.BlockSpec(shape, lambda bi, i: (0,) * len(shape))
    resident = lambda shape: pl.BlockSpec(shape, lambda bi, i: (0,) * len(shape), pipeline_mode=pl.Buffered(1))
    return pl.pallas_call(
        functools.partial(_ffn_kernel, d_ff=d_ff),
        grid=(b, s // tm),
        in_specs=[
            pl.BlockSpec((1, halo, d), lambda bi, i: (bi, jnp.maximum(i * per_tile - 1, 0), 0)),
            pl.BlockSpec((1, tm, d), lambda bi, i: (bi, i, 0)),
            pl.BlockSpec((1, halo, d), lambda bi, i: (bi, jnp.minimum((i + 1) * per_tile, n_halo_blocks - 1), 0)),
            full(g2.shape), resident(w_up.shape), full(conv_w.shape), full(conv_b.shape), resident(w_down.shape),
        ],
        out_specs=pl.BlockSpec((1, tm, d), lambda bi, i: (bi, i, 0)),
        out_shape=jax.ShapeDtypeStruct((b, s, d), F32),
        scratch_shapes=[pltpu.VMEM((tm + 2 * halo, d), BF16), pltpu.VMEM((tm, d_ff), BF16)],
        compiler_params=pltpu.CompilerParams(
            dimension_semantics=("parallel", "parallel"), vmem_limit_bytes=VMEM_LIMIT_BYTES),
        name="ffn",
    )(x, x, x, g2, w_up, conv_w, conv_b, w_down)


def _rope_tables_t(seq_len):
    rows = seq_len // GRID_W
    row = jnp.repeat(jnp.arange(rows, dtype=F32), GRID_W)
    col = jnp.tile(jnp.arange(GRID_W, dtype=F32), rows)
    inv = ROPE_THETA ** (-jnp.arange(0, ROPE_AXIS_DIM, 2, dtype=F32) / ROPE_AXIS_DIM)
    ang_r = (row[:, None] * inv[None, :]).T
    ang_c = (col[:, None] * inv[None, :]).T
    cos_t = jnp.concatenate([jnp.cos(ang_r)] * 2 + [jnp.cos(ang_c)] * 2, axis=0)
    sin_t = jnp.concatenate([-jnp.sin(ang_r), jnp.sin(ang_r), -jnp.sin(ang_c), jnp.sin(ang_c)], axis=0)
    return cos_t, sin_t


def kernel(x, norm1_g, w_in, qa_norm_g, ka_norm_g, qb_norm_g, kb_norm_g, outa_norm_g, outb_norm_g,
           w_out, norm2_g, w_up, conv_w, conv_b, w_down):
    b, s, d = x.shape
    scale = HEAD_DIM ** -0.5 * math.log2(math.e)
    a_cols = WIDTH_A + 2 * KV_WIDTH_A
    wa = w_in[:, :a_cols].astype(BF16)
    wb = w_in[:, a_cols:].astype(BF16)
    ga = jnp.concatenate([jnp.tile(qa_norm_g, N_HEADS_A) * scale, jnp.tile(ka_norm_g, N_KV_A)])
    ga = jnp.broadcast_to(ga[:, None], (ga.shape[0], LANES))
    gqb = (jnp.tile(qb_norm_g, N_HEADS_B) * scale)[None, :]
    gkb = jnp.tile(kb_norm_g, N_HEADS_B)[None, :]
    cos_t, sin_t = _rope_tables_t(s)

    dilations = [dil for _, dil in DILATED_PATTERNS]
    assert dilations[0] == 1
    grouped_dilations = dilations[1:]
    qt, k, vt, *qkv_b = _proj(x, norm1_g[None, :], wa, wb, ga, cos_t, sin_t, gqb, gkb, grouped_dilations)

    def scores_bounded(gq, gk):
        bound = math.sqrt(HEAD_DIM) * jnp.max(jnp.abs(gq)) * jnp.max(jnp.abs(gk))
        return (bound <= SAFE_SCORE_BOUND).astype(jnp.int32).reshape(1)

    oat = _attn_a(scores_bounded(qa_norm_g, ka_norm_g), qt, k, vt)
    bounded_b = scores_bounded(qb_norm_g, kb_norm_g)
    results = [_dilated_rows(bounded_b, *qkv_b[:3])]
    for n, dil in enumerate(grouped_dilations):
        results.append(_dilated_grouped(bounded_b, *qkv_b[3 * (n + 1):3 * (n + 2)], dil))
    obs, lses = zip(*results)
    wo = w_out.astype(BF16)
    x2 = _merge(x, oat, obs, lses, outa_norm_g[None, :], outb_norm_g[None, :], wo[:WIDTH_A], wo[WIDTH_A:],
                grouped_dilations)
    return _ffn(x2, norm2_g[None, :], w_up.astype(BF16), conv_w, conv_b[None, :], w_down.astype(BF16))
```

```python
import functools
import math

import jax
import jax.numpy as jnp
from jax import lax
from jax.experimental import pallas as pl
from jax.experimental.pallas import tpu as pltpu

F32 = jnp.float32
BF16 = jnp.bfloat16
FP8 = jnp.float8_e4m3fn

HEAD_DIM = 64
N_HEADS_A = 8
N_KV_A = 2
GQA_GROUP = N_HEADS_A // N_KV_A
N_HEADS_B = 8
WIDTH_A = N_HEADS_A * HEAD_DIM
WIDTH_B = N_HEADS_B * HEAD_DIM
KV_WIDTH_A = N_KV_A * HEAD_DIM
GRID_W = 64
ROPE_THETA = 10000.0
ROPE_AXIS_DIM = HEAD_DIM // 2
DIL_BLOCK = 64
DILATED_PATTERNS = ((128, 1), (512, 4), (2048, 16))
NORM_EPS = 1e-6
NEG_INF = -1e30

LANES = 128
SUBLANES = 8
MXU_DIM = 256
VMEM_LIMIT_BYTES = 56 * 1024 * 1024

PROJ_TM = 1024
GROUP_TILE = 512
GROUP_STEP = 4
ATT_TQ = 2048
ATT_TK = 1024
SAFE_SCORE_BOUND = 40.0
VT_ROWS = HEAD_DIM + 16
DIL_TL = 512
DIL_SUB = 128
DIL_UNROLL = 4
FFN_TM = 1024
FFN_FC = 256
FFN_HALO = 16
LSE_LANES = 16


def _nt_dot(a, b):
    return lax.dot_general(a, b, (((1,), (1,)), ((), ())), preferred_element_type=F32)


def _proj_kernel(x_ref, g1_ref, wa_ref, wb_ref, ga_ref, cos_ref, sin_ref, gqb_ref, gkb_ref,
                 qt_ref, k_ref, vt_ref, qb_ref, kb_ref, vb_ref, *rest):
    *grouped_refs, nat_all_ref, lvl_all_ref = rest
    for part in range(x_ref.shape[1] // GROUP_TILE):
        _proj_rows(part * GROUP_TILE, x_ref, g1_ref, wa_ref, wb_ref, ga_ref, cos_ref, sin_ref, gqb_ref, gkb_ref,
                   qt_ref, k_ref, vt_ref, qb_ref, kb_ref, vb_ref, grouped_refs,
                   nat_all_ref.at[part], lvl_all_ref.at[part])


def _proj_rows(r0, x_ref, g1_ref, wa_ref, wb_ref, ga_ref, cos_ref, sin_ref, gqb_ref, gkb_ref,
               qt_ref, k_ref, vt_ref, qb_ref, kb_ref, vb_ref, grouped_refs, nat_ref, lvl_ref):
    tm = GROUP_TILE
    rows = slice(r0, r0 + tm)
    x = x_ref[0, rows]
    ms = jnp.mean(x * x, axis=-1, keepdims=True)
    hn = (x * lax.rsqrt(ms + NORM_EPS) * g1_ref[...]).astype(BF16)
    ya = jnp.dot(hn, wa_ref[...], preferred_element_type=F32)
    yb = jnp.dot(hn, wb_ref[...], preferred_element_type=F32)

    ya_t = ya.T
    ones_rows = jnp.where(lax.broadcasted_iota(jnp.int32, (VT_ROWS - HEAD_DIM, tm), 0) == 0, 1.0, 0.0).astype(BF16)
    for g in range(N_KV_A):
        v_rows = ya_t[WIDTH_A + KV_WIDTH_A + HEAD_DIM * g:WIDTH_A + KV_WIDTH_A + HEAD_DIM * (g + 1)]
        vt_ref[0, g, :, rows] = jnp.concatenate([v_rows.astype(BF16), ones_rows], axis=0)
    cos_t = cos_ref[:, rows]
    sin_t = sin_ref[:, rows]
    reps = tm // LANES
    zeros = jnp.zeros((HEAD_DIM, tm), F32)
    q16 = ROPE_AXIS_DIM // 2

    def limbs8(val):
        hi = val.astype(FP8).astype(F32)
        return hi, (val - hi).astype(FP8).astype(F32)

    for h in range(N_HEADS_A + N_KV_A):
        u = ya_t[HEAD_DIM * h:HEAD_DIM * (h + 1)]
        r = lax.rsqrt(jnp.mean(u * u, axis=0, keepdims=True) + NORM_EPS)
        g = jnp.tile(ga_ref[HEAD_DIM * h:HEAD_DIM * (h + 1), :], (1, reps))
        un = u * r * g
        partner = jnp.concatenate(
            [un[q16:2 * q16], un[0:q16], un[3 * q16:4 * q16], un[2 * q16:3 * q16]], axis=0)
        hi, lo = limbs8(un * cos_t + partner * sin_t)
        if h < N_HEADS_A:
            qt_ref[0, h, :, rows] = jnp.concatenate([hi, lo, hi, zeros], axis=0).astype(FP8)
        else:
            k_ref[0, h - N_HEADS_A, rows] = jnp.concatenate([hi, hi, lo, zeros], axis=0).T.astype(FP8)

    ri = lax.broadcasted_iota(jnp.int32, (MXU_DIM, MXU_DIM), 0) // HEAD_DIM
    ci = lax.broadcasted_iota(jnp.int32, (MXU_DIM, MXU_DIM), 1) // HEAD_DIM
    ones_bd = jnp.where(ri == ci, 1.0, 0.0).astype(BF16)

    def head_norm(y, g_ref):
        sq = (y * y).astype(BF16)
        ss = jnp.concatenate(
            [jnp.dot(sq[:, c:c + MXU_DIM], ones_bd, preferred_element_type=F32)
             for c in range(0, y.shape[1], MXU_DIM)], axis=1)
        return y * lax.rsqrt(ss * (1.0 / HEAD_DIM) + NORM_EPS) * g_ref[...]

    parts = (head_norm(yb[:, :WIDTH_B], gqb_ref), head_norm(yb[:, WIDTH_B:2 * WIDTH_B], gkb_ref),
             yb[:, 2 * WIDTH_B:])
    slabs = WIDTH_B // LANES
    for c, (part, ref) in enumerate(zip(parts, (qb_ref, kb_ref, vb_ref))):
        ref[0, rows] = part.astype(BF16)
        for jj in range(slabs):
            nat_ref[slabs * c + jj] = part[:, LANES * jj:LANES * (jj + 1)]

    src_ref, block = nat_ref, tm
    n_levels = len(grouped_refs) // 3
    for n in range(n_levels):
        sub = block // GROUP_STEP
        for j in range(3 * slabs):
            c, jj = divmod(j, slabs)
            dst_ref = grouped_refs[3 * n + c]
            for lo in range(0, tm, sub):
                blk0, r = (lo // block) * block, (lo % block) // sub
                picked = src_ref[j, pl.ds(blk0 + r, sub, stride=GROUP_STEP), :]
                if n + 1 < n_levels:
                    lvl_ref[j, lo:lo + sub, :] = picked
                dst_ref[0, r0 + lo:r0 + lo + sub, LANES * jj:LANES * (jj + 1)] = picked.astype(BF16)
        src_ref, block = lvl_ref, sub


def _group_index(tau, tm, dilation):
    base, size, local = 0, tm, tau
    while dilation > 1:
        size //= GROUP_STEP
        base = base + (local % GROUP_STEP) * size
        local = local // GROUP_STEP
        dilation //= GROUP_STEP
    return base + local


def _proj(x, g1, wa, wb, ga, cos_t, sin_t, gqb, gkb, grouped_dilations):
    b, s, d = x.shape
    tm = PROJ_TM
    assert tuple(grouped_dilations) == tuple(GROUP_STEP ** (n + 1) for n in range(len(grouped_dilations)))
    assert len(grouped_dilations) <= 2
    full = lambda shape: pl.BlockSpec(shape, lambda bi, i: (0,) * len(shape))
    rows_b = jax.ShapeDtypeStruct((b, s, WIDTH_B), BF16)
    out_shape = (
        jax.ShapeDtypeStruct((b, N_HEADS_A, MXU_DIM, s), FP8),
        jax.ShapeDtypeStruct((b, N_KV_A, s, MXU_DIM), FP8),
        jax.ShapeDtypeStruct((b, N_KV_A, VT_ROWS, s), BF16),
    ) + (rows_b,) * (3 + 3 * len(grouped_dilations))
    row = lambda w: pl.BlockSpec((1, tm, w), lambda bi, i: (bi, i, 0))
    return pl.pallas_call(
        _proj_kernel,
        grid=(b, s // tm),
        in_specs=[
            row(d), full(g1.shape), full(wa.shape), full(wb.shape), full(ga.shape),
            pl.BlockSpec((HEAD_DIM, tm), lambda bi, i: (0, i)),
            pl.BlockSpec((HEAD_DIM, tm), lambda bi, i: (0, i)),
            full(gqb.shape), full(gkb.shape),
        ],
        out_specs=(
            pl.BlockSpec((1, N_HEADS_A, MXU_DIM, tm), lambda bi, i: (bi, 0, 0, i)),
            pl.BlockSpec((1, N_KV_A, tm, MXU_DIM), lambda bi, i: (bi, 0, i, 0)),
            pl.BlockSpec((1, N_KV_A, VT_ROWS, tm), lambda bi, i: (bi, 0, 0, i)),
        ) + (row(WIDTH_B),) * (3 + 3 * len(grouped_dilations)),
        out_shape=out_shape,
        scratch_shapes=[pltpu.VMEM((tm // GROUP_TILE, 3 * WIDTH_B // LANES, GROUP_TILE, LANES), F32)] * 2,
        compiler_params=pltpu.CompilerParams(
            dimension_semantics=("parallel", "parallel"), vmem_limit_bytes=VMEM_LIMIT_BYTES),
        name="proj",
    )(x, g1, wa, wb, ga, cos_t, sin_t, gqb, gkb)


def _attn_a_kernel(bounded_ref, rescale_ref, qt_ref, k_ref, vt_ref, ot_ref, m_ref, acc_ref):
    j = pl.program_id(3)
    bounded = bounded_ref[0] != 0

    @pl.when(j == 0)
    def _():
        m_ref[...] = jnp.full(m_ref.shape, NEG_INF, F32)
        acc_ref[...] = jnp.zeros(acc_ref.shape, F32)

    k = k_ref[0, 0]
    vt = vt_ref[0, 0]

    def scores(h):
        return jnp.dot(k, qt_ref[0, h], preferred_element_type=F32) * rescale_ref[0]

    @pl.when(bounded)
    def _():
        tk, tq = k_ref.shape[2], qt_ref.shape[3]
        for h in range(GQA_GROUP):
            for c in range(0, tq, MXU_DIM):
                q_c = qt_ref[0, h, :, c:c + MXU_DIM]
                part = None
                for r in range(0, tk, MXU_DIM):
                    s_t = jnp.dot(k_ref[0, 0, r:r + MXU_DIM, :], q_c, preferred_element_type=F32)
                    pv = jnp.dot(vt_ref[0, 0, :, r:r + MXU_DIM], jnp.exp2(s_t).astype(BF16),
                                 preferred_element_type=F32)
                    part = pv if part is None else part + pv
                acc_ref[h, :, c:c + MXU_DIM] += part

    @pl.when(jnp.logical_not(bounded))
    def _():
        for h in range(GQA_GROUP):
            s_t = scores(h)
            m_prev = m_ref[h]
            m_new = jnp.maximum(m_prev, jnp.max(s_t, axis=0, keepdims=True))
            p_t = jnp.exp2(s_t - m_new).astype(BF16)
            acc_ref[h] = (jnp.exp2(m_prev - m_new) * acc_ref[h]
                          + jnp.dot(vt, p_t, preferred_element_type=F32))
            m_ref[h] = m_new

    @pl.when(j == pl.num_programs(3) - 1)
    def _():
        for h in range(GQA_GROUP):
            a = acc_ref[h]
            ot_ref[0, HEAD_DIM * h:HEAD_DIM * (h + 1), :] = (
                a[:HEAD_DIM] * (1.0 / a[HEAD_DIM:HEAD_DIM + 1])).astype(BF16)


def _attn_a(bounded, rescale, qt, k, vt):
    b, _, _, s = qt.shape
    tq, tk = ATT_TQ, ATT_TK
    gw = GQA_GROUP * HEAD_DIM
    return pl.pallas_call(
        _attn_a_kernel,
        grid=(b, N_KV_A, s // tq, s // tk),
        in_specs=[
            pl.BlockSpec(memory_space=pltpu.SMEM),
            pl.BlockSpec(memory_space=pltpu.SMEM),
            pl.BlockSpec((1, GQA_GROUP, MXU_DIM, tq), lambda bi, g, i, j: (bi, g, 0, i)),
            pl.BlockSpec((1, 1, tk, MXU_DIM), lambda bi, g, i, j: (bi, g, j, 0)),
            pl.BlockSpec((1, 1, VT_ROWS, tk), lambda bi, g, i, j: (bi, g, 0, j)),
        ],
        out_specs=pl.BlockSpec((1, gw, tq), lambda bi, g, i, j: (bi, g, i)),
        out_shape=jax.ShapeDtypeStruct((b, WIDTH_A, s), BF16),
        scratch_shapes=[
            pltpu.VMEM((GQA_GROUP, 1, tq), F32),
            pltpu.VMEM((GQA_GROUP, VT_ROWS, tq), F32),
        ],
        compiler_params=pltpu.CompilerParams(
            dimension_semantics=("parallel", "parallel", "parallel", "arbitrary"),
            vmem_limit_bytes=VMEM_LIMIT_BYTES),
        name="attn_a",
    )(bounded, rescale, qt, k, vt)


def _dil_bias_init(bias_ref, dilation):
    tk = DIL_SUB + 2 * DIL_BLOCK
    qi = lax.broadcasted_iota(jnp.int32, (DIL_SUB, tk), 0)
    kj = lax.broadcasted_iota(jnp.int32, (DIL_SUB, tk), 1)
    aoff = jnp.abs(kj - DIL_BLOCK - qi)
    dist = (aoff * dilation).astype(F32)
    for h in range(N_HEADS_B):
        slope = 2.0 ** (-8.0 * (h + 1) / N_HEADS_B) * math.log2(math.e)
        bias_ref[h] = jnp.where(aoff <= DIL_BLOCK, -slope * dist, NEG_INF)


def _lse_lane_pair(lane):
    return (lane % HEAD_DIM) // LSE_LANES


def _lse_lane_of_head(h):
    return HEAD_DIM * (h % 2) + LSE_LANES * (h // 2)


def _dil_sub_tiles(bounded, q_ref, kext_ref, vext_ref, o_ref, lse_ref, bias_ref, *, n_sub, first_key, length):
    tk = DIL_SUB + 2 * DIL_BLOCK
    first_head = lax.broadcasted_iota(jnp.int32, (DIL_SUB, LANES), 1) < HEAD_DIM
    first_head_k = lax.broadcasted_iota(jnp.int32, (tk, LANES), 1) < HEAD_DIM
    lse_owner = _lse_lane_pair(lax.broadcasted_iota(jnp.int32, (DIL_SUB, LANES), 1))
    key_lane = lax.broadcasted_iota(jnp.int32, (1, tk), 1)
    stack_row = lax.broadcasted_iota(jnp.int32, (2 * tk, LANES), 0) < tk
    stack_lane = lax.broadcasted_iota(jnp.int32, (2 * tk, LANES), 1) < HEAD_DIM
    ones_stack = jnp.where(stack_row == stack_lane, 1.0, 0.0).astype(BF16)

    def sub_tile(st, carry, *, use_max):
        r0 = pl.multiple_of(st * DIL_SUB, DIL_SUB)
        kidx = first_key + r0 + key_lane
        in_range = (kidx >= 0) & (kidx < length)
        lse_tile = jnp.zeros((DIL_SUB, LANES), F32)
        for hp in range(N_HEADS_B // 2):
            cols = slice(LANES * hp, LANES * (hp + 1))
            qp = q_ref[pl.ds(r0, DIL_SUB), cols]
            kp = kext_ref[pl.ds(r0, tk), cols]
            vp = vext_ref[pl.ds(r0, tk), cols]
            zeros_v = jnp.zeros_like(vp)
            v_stack = jnp.concatenate([jnp.where(first_head_k, vp, zeros_v),
                                       jnp.where(first_head_k, zeros_v, vp)], axis=0)
            ps, ms = [], []
            for e in range(2):
                own = first_head if e == 0 else jnp.logical_not(first_head)
                qm = jnp.where(own, qp, jnp.zeros_like(qp))
                s = _nt_dot(qm, kp) + bias_ref[2 * hp + e]
                s = jnp.where(in_range, s, NEG_INF)
                if use_max:
                    ms.append(jnp.max(s, axis=-1, keepdims=True))
                    s = s - ms[-1]
                ps.append(jnp.exp2(s).astype(BF16))
            p_pair = jnp.concatenate(ps, axis=1)
            both = jnp.dot(p_pair, jnp.concatenate([v_stack, ones_stack], axis=1), preferred_element_type=F32)
            pv, l_pair = both[:, :LANES], both[:, LANES:]
            o_ref[pl.ds(r0, DIL_SUB), cols] = (pv * (1.0 / l_pair)).astype(BF16)
            lse = jnp.log(l_pair)
            if use_max:
                lse = lse + jnp.where(first_head, ms[0], ms[1]) * math.log(2.0)
            lse_tile = jnp.where(lse_owner == hp, lse, lse_tile)
        lse_ref[pl.ds(r0, DIL_SUB), :] = lse_tile
        return carry

    @pl.when(bounded)
    def _():
        lax.fori_loop(0, n_sub, functools.partial(sub_tile, use_max=False), 0, unroll=DIL_UNROLL)

    @pl.when(jnp.logical_not(bounded))
    def _():
        lax.fori_loop(0, n_sub, functools.partial(sub_tile, use_max=True), 0)


def _dil_rows_kernel(bounded_ref, q_ref, kp_ref, kc_ref, kn_ref, vp_ref, vc_ref, vn_ref, o_ref, lse_ref,
                     kext_ref, vext_ref, bias_ref, *, length):
    tl = q_ref.shape[1]
    bi, i = pl.program_id(0), pl.program_id(1)

    @pl.when((bi == 0) & (i == 0))
    def _():
        _dil_bias_init(bias_ref, 1)

    kext_ref[0:DIL_BLOCK] = kp_ref[0]
    kext_ref[DIL_BLOCK:DIL_BLOCK + tl] = kc_ref[0]
    kext_ref[DIL_BLOCK + tl:] = kn_ref[0]
    vext_ref[0:DIL_BLOCK] = vp_ref[0]
    vext_ref[DIL_BLOCK:DIL_BLOCK + tl] = vc_ref[0]
    vext_ref[DIL_BLOCK + tl:] = vn_ref[0]
    _dil_sub_tiles(bounded_ref[0] != 0, q_ref.at[0], kext_ref, vext_ref, o_ref.at[0], lse_ref.at[0], bias_ref,
                   n_sub=tl // DIL_SUB, first_key=i * tl - DIL_BLOCK, length=length)


def _dilated_rows(bounded, q, k, v):
    b, s, w = q.shape
    tl = DIL_TL
    blocks_per_tile = tl // DIL_BLOCK
    n_blocks = s // DIL_BLOCK
    cur = pl.BlockSpec((1, tl, w), lambda bi, i: (bi, i, 0))
    prev = pl.BlockSpec((1, DIL_BLOCK, w), lambda bi, i: (bi, jnp.maximum(i * blocks_per_tile - 1, 0), 0))
    nxt = pl.BlockSpec((1, DIL_BLOCK, w),
                       lambda bi, i: (bi, jnp.minimum((i + 1) * blocks_per_tile, n_blocks - 1), 0))
    return pl.pallas_call(
        functools.partial(_dil_rows_kernel, length=s),
        grid=(b, s // tl),
        in_specs=[pl.BlockSpec(memory_space=pltpu.SMEM), cur, prev, cur, nxt, prev, cur, nxt],
        out_specs=(cur, pl.BlockSpec((1, tl, LANES), lambda bi, i: (bi, i, 0))),
        out_shape=(jax.ShapeDtypeStruct((b, s, w), BF16), jax.ShapeDtypeStruct((b, s, LANES), F32)),
        scratch_shapes=[
            pltpu.VMEM((tl + 2 * DIL_BLOCK, w), BF16),
            pltpu.VMEM((tl + 2 * DIL_BLOCK, w), BF16),
            pltpu.VMEM((N_HEADS_B, DIL_SUB, DIL_SUB + 2 * DIL_BLOCK), F32),
        ],
        compiler_params=pltpu.CompilerParams(
            dimension_semantics=("arbitrary", "arbitrary"), vmem_limit_bytes=VMEM_LIMIT_BYTES),
        name="dil_1",
    )(bounded, q, k, k, k, v, v, v)


def _dil_grouped_kernel(bounded_ref, q_ref, k_ref, v_ref, o_ref, lse_ref,
                        qflat_ref, kext_ref, vext_ref, oflat_ref, lseflat_ref, bias_ref, *, dilation):
    n_tiles, rows = q_ref.shape[1], q_ref.shape[2]
    length = n_tiles * rows
    bi, r = pl.program_id(0), pl.program_id(1)

    @pl.when((bi == 0) & (r == 0))
    def _():
        _dil_bias_init(bias_ref, dilation)

    border = jnp.zeros((DIL_BLOCK, kext_ref.shape[1]), BF16)
    for ext_ref in (kext_ref, vext_ref):
        ext_ref[0:DIL_BLOCK] = border
        ext_ref[DIL_BLOCK + length:] = border
    for t in range(n_tiles):
        qflat_ref[rows * t:rows * (t + 1)] = q_ref[0, t]
        kext_ref[DIL_BLOCK + rows * t:DIL_BLOCK + rows * (t + 1)] = k_ref[0, t]
        vext_ref[DIL_BLOCK + rows * t:DIL_BLOCK + rows * (t + 1)] = v_ref[0, t]
    _dil_sub_tiles(bounded_ref[0] != 0, qflat_ref, kext_ref, vext_ref, oflat_ref, lseflat_ref, bias_ref,
                   n_sub=length // DIL_SUB, first_key=-DIL_BLOCK, length=length)
    for t in range(n_tiles):
        o_ref[0, t] = oflat_ref[rows * t:rows * (t + 1)]
        lse_ref[0, t] = lseflat_ref[rows * t:rows * (t + 1)]


def _dilated_grouped(bounded, q, k, v, dilation):
    b, s, w = q.shape
    n_tiles, rows = s // GROUP_TILE, GROUP_TILE // dilation
    length = n_tiles * rows
    view = lambda a: a.reshape(b, n_tiles, dilation, rows, a.shape[-1])
    spec = lambda width: pl.BlockSpec((1, n_tiles, None, rows, width), lambda bi, r: (bi, 0, r, 0, 0))
    o, lse = pl.pallas_call(
        functools.partial(_dil_grouped_kernel, dilation=dilation),
        grid=(b, dilation),
        in_specs=[pl.BlockSpec(memory_space=pltpu.SMEM), spec(w), spec(w), spec(w)],
        out_specs=(spec(w), spec(LANES)),
        out_shape=(jax.ShapeDtypeStruct((b, n_tiles, dilation, rows, w), BF16),
                   jax.ShapeDtypeStruct((b, n_tiles, dilation, rows, LANES), F32)),
        scratch_shapes=[
            pltpu.VMEM((length, w), BF16),
            pltpu.VMEM((length + 2 * DIL_BLOCK, w), BF16),
            pltpu.VMEM((length + 2 * DIL_BLOCK, w), BF16),
            pltpu.VMEM((length, w), BF16),
            pltpu.VMEM((length, LANES), F32),
            pltpu.VMEM((N_HEADS_B, DIL_SUB, DIL_SUB + 2 * DIL_BLOCK), F32),
        ],
        compiler_params=pltpu.CompilerParams(
            dimension_semantics=("arbitrary", "arbitrary"), vmem_limit_bytes=VMEM_LIMIT_BYTES),
        name=f"dil_{dilation}",
    )(bounded, view(q), view(k), view(v))
    return o.reshape(b, s, w), lse.reshape(b, s, LANES)


def _ungroup_rows(val, levels, scratch_refs):
    slabs = val.shape[1] // LANES
    for lvl in reversed(range(levels)):
        block = GROUP_TILE // GROUP_STEP ** lvl
        sub = block // GROUP_STEP
        ref = scratch_refs[lvl]
        for j in range(slabs):
            for lo in range(0, GROUP_TILE, sub):
                blk0, r = (lo // block) * block, (lo % block) // sub
                ref[j, pl.ds(blk0 + r, sub, stride=GROUP_STEP), :] = val[lo:lo + sub, LANES * j:LANES * (j + 1)]
        val = jnp.concatenate([ref[j] for j in range(slabs)], axis=1)
    return val


def _merge_kernel(x_ref, oat_ref, o1_ref, o2_ref, o3_ref, l1_ref, l2_ref, l3_ref,
                  ga_ref, gb_ref, wa_ref, wb_ref, out_ref, *scratch):
    def group_norm(y, g_ref):
        ms = jnp.mean(y * y, axis=-1, keepdims=True)
        return (y * lax.rsqrt(ms + NORM_EPS) * g_ref[...]).astype(BF16)

    oa = oat_ref[0].astype(F32).T

    o2_scr, o3_scr, l2_scr, l3_scr = scratch[0:1], scratch[1:3], scratch[3:4], scratch[4:6]

    def limbs(val):
        hi = val.astype(BF16)
        return hi, (val - hi.astype(F32)).astype(BF16)

    l1, l2, l3 = l1_ref[0], _ungroup_rows(l2_ref[0], 1, l2_scr), _ungroup_rows(l3_ref[0], 2, l3_scr)
    mx = jnp.maximum(jnp.maximum(l1, l2), l3)
    e1, e2, e3 = jnp.exp(l1 - mx), jnp.exp(l2 - mx), jnp.exp(l3 - mx)
    inv = 1.0 / (e1 + e2 + e3)
    ri = lax.broadcasted_iota(jnp.int32, (2 * LANES, WIDTH_B), 0) % LANES
    ci = lax.broadcasted_iota(jnp.int32, (2 * LANES, WIDTH_B), 1)
    expand = jnp.where(ri == _lse_lane_of_head(ci // HEAD_DIM), 1.0, 0.0).astype(BF16)

    def widen(wgt):
        return jnp.dot(jnp.concatenate(limbs(wgt), axis=1), expand, preferred_element_type=F32)

    ob = (widen(e1 * inv) * o1_ref[0].astype(F32)
          + widen(e2 * inv) * _ungroup_rows(o2_ref[0].astype(F32), 1, o2_scr)
          + widen(e3 * inv) * _ungroup_rows(o3_ref[0].astype(F32), 2, o3_scr))
    mixed_a = group_norm(oa, ga_ref)
    mixed_b = group_norm(ob, gb_ref)
    out_ref[0] = (x_ref[0]
                  + jnp.dot(mixed_a, wa_ref[...], preferred_element_type=F32)
                  + jnp.dot(mixed_b, wb_ref[...], preferred_element_type=F32))


def _merge(x, oat, obs, lses, ga, gb, wo_a, wo_b, grouped_dilations):
    b, s, d = x.shape
    tm = GROUP_TILE
    assert tuple(grouped_dilations) == (GROUP_STEP, GROUP_STEP ** 2)
    slab = lambda n: pltpu.VMEM((n, tm, LANES), F32)
    wide, narrow = WIDTH_B // LANES, 1
    full = lambda shape: pl.BlockSpec(shape, lambda bi, i: (0,) * len(shape))
    row = lambda w: pl.BlockSpec((1, tm, w), lambda bi, i: (bi, i, 0))
    return pl.pallas_call(
        _merge_kernel,
        grid=(b, s // tm),
        in_specs=[row(d), pl.BlockSpec((1, WIDTH_A, tm), lambda bi, i: (bi, 0, i)),
                  row(WIDTH_B), row(WIDTH_B), row(WIDTH_B), row(LANES), row(LANES), row(LANES),
                  full(ga.shape), full(gb.shape), full(wo_a.shape), full(wo_b.shape)],
        out_specs=row(d),
        out_shape=jax.ShapeDtypeStruct((b, s, d), F32),
        scratch_shapes=[slab(wide), slab(wide), slab(wide), slab(narrow), slab(narrow), slab(narrow)],
        compiler_params=pltpu.CompilerParams(
            dimension_semantics=("parallel", "parallel"), vmem_limit_bytes=VMEM_LIMIT_BYTES),
        name="merge",
    )(x, oat, *obs, *lses, ga, gb, wo_a, wo_b)


def _gelu_tanh(x):
    c = math.sqrt(2.0 / math.pi)
    return 0.5 * x * (1.0 + jnp.tanh(c * (x + 0.044715 * (x * x * x))))


def _ffn_kernel(xp_ref, xc_ref, xn_ref, g2_ref, wup_ref, cw_ref, cb_ref, wdn_ref, out_ref,
                hext_ref, act_ref, *, d_ff):
    tm = xc_ref.shape[1]
    halo = xp_ref.shape[1]
    i = pl.program_id(1)
    rows = tm + 2 * halo

    def normed(x):
        ms = jnp.mean(x * x, axis=-1, keepdims=True)
        return x * lax.rsqrt(ms + NORM_EPS) * g2_ref[...]

    keep_prev = (i > 0).astype(F32)
    keep_next = (i < pl.num_programs(1) - 1).astype(F32)
    hext_ref[0:halo] = (normed(xp_ref[0]) * keep_prev).astype(BF16)
    hext_ref[halo:halo + tm] = normed(xc_ref[0]).astype(BF16)
    hext_ref[halo + tm:] = (normed(xn_ref[0]) * keep_next).astype(BF16)
    hext = hext_ref[...]

    def conv(u, col):
        w = cw_ref[:, col:col + FFN_FC]
        up = pltpu.roll(u, 1, 0)
        dn = pltpu.roll(u, rows - 1, 0)
        y = up * w[0:1] + u * w[1:2] + dn * w[2:3] + cb_ref[:, col:col + FFN_FC]
        return y[halo:halo + tm]

    for c in range(0, d_ff, FFN_FC):
        ug = jnp.dot(hext, wup_ref[:, c:c + FFN_FC], preferred_element_type=F32)
        uv = jnp.dot(hext, wup_ref[:, d_ff + c:d_ff + c + FFN_FC], preferred_element_type=F32)
        act_ref[:, c:c + FFN_FC] = (_gelu_tanh(conv(ug, c)) * conv(uv, d_ff + c)).astype(BF16)

    out_ref[0] = xc_ref[0] + jnp.dot(act_ref[...], wdn_ref[...], preferred_element_type=F32)


def _ffn(x, g2, w_up, conv_w, conv_b, w_down):
    b, s, d = x.shape
    d_ff = w_down.shape[0]
    tm, halo = FFN_TM, FFN_HALO
    per_tile = tm // halo
    n_halo_blocks = s // halo
    full = lambda shape: pl.BlockSpec(shape, lambda bi, i: (0,) * len(shape))
    resident = lambda shape: pl.BlockSpec(shape, lambda bi, i: (0,) * len(shape), pipeline_mode=pl.Buffered(1))
    return pl.pallas_call(
        functools.partial(_ffn_kernel, d_ff=d_ff),
        grid=(b, s // tm),
        in_specs=[
            pl.BlockSpec((1, halo, d), lambda bi, i: (bi, jnp.maximum(i * per_tile - 1, 0), 0)),
            pl.BlockSpec((1, tm, d), lambda bi, i: (bi, i, 0)),
            pl.BlockSpec((1, halo, d), lambda bi, i: (bi, jnp.minimum((i + 1) * per_tile, n_halo_blocks - 1), 0)),
            full(g2.shape), resident(w_up.shape), full(conv_w.shape), full(conv_b.shape), resident(w_down.shape),
        ],
        out_specs=pl.BlockSpec((1, tm, d), lambda bi, i: (bi, i, 0)),
        out_shape=jax.ShapeDtypeStruct((b, s, d), F32),
        scratch_shapes=[pltpu.VMEM((tm + 2 * halo, d), BF16), pltpu.VMEM((tm, d_ff), BF16)],
        compiler_params=pltpu.CompilerParams(
            dimension_semantics=("parallel", "parallel"), vmem_limit_bytes=VMEM_LIMIT_BYTES),
        name="ffn",
    )(x, x, x, g2, w_up, conv_w, conv_b, w_down)


def _rope_tables_t(seq_len):
    rows = seq_len // GRID_W
    row = jnp.repeat(jnp.arange(rows, dtype=F32), GRID_W)
    col = jnp.tile(jnp.arange(GRID_W, dtype=F32), rows)
    inv = ROPE_THETA ** (-jnp.arange(0, ROPE_AXIS_DIM, 2, dtype=F32) / ROPE_AXIS_DIM)
    ang_r = (row[:, None] * inv[None, :]).T
    ang_c = (col[:, None] * inv[None, :]).T
    cos_t = jnp.concatenate([jnp.cos(ang_r)] * 2 + [jnp.cos(ang_c)] * 2, axis=0)
    sin_t = jnp.concatenate([-jnp.sin(ang_r), jnp.sin(ang_r), -jnp.sin(ang_c), jnp.sin(ang_c)], axis=0)
    return cos_t, sin_t


def _score_bound(gq, gk):
    return math.sqrt(HEAD_DIM) * jnp.max(jnp.abs(gq)) * jnp.max(jnp.abs(gk))


def kernel(x, norm1_g, w_in, qa_norm_g, ka_norm_g, qb_norm_g, kb_norm_g, outa_norm_g, outb_norm_g,
           w_out, norm2_g, w_up, conv_w, conv_b, w_down):
    b, s, d = x.shape
    scale = HEAD_DIM ** -0.5 * math.log2(math.e)
    a_cols = WIDTH_A + 2 * KV_WIDTH_A
    wa = w_in[:, :a_cols].astype(BF16)
    wb = w_in[:, a_cols:].astype(BF16)
    bound_a = _score_bound(qa_norm_g, ka_norm_g)
    bounded_a = (bound_a <= SAFE_SCORE_BOUND).astype(jnp.int32).reshape(1)
    rescale_a = jnp.maximum(bound_a / SAFE_SCORE_BOUND, 1.0)
    positive = lambda g: jnp.where(g > 0, g, 1.0)
    gq_max, gk_max = positive(jnp.max(jnp.abs(qa_norm_g))), positive(jnp.max(jnp.abs(ka_norm_g)))
    ga = jnp.concatenate([jnp.tile(qa_norm_g, N_HEADS_A) * jnp.sqrt(scale * gk_max / (gq_max * rescale_a)),
                          jnp.tile(ka_norm_g, N_KV_A) * jnp.sqrt(scale * gq_max / (gk_max * rescale_a))])
    ga = jnp.broadcast_to(ga[:, None], (ga.shape[0], LANES))
    gqb = (jnp.tile(qb_norm_g, N_HEADS_B) * scale)[None, :]
    gkb = jnp.tile(kb_norm_g, N_HEADS_B)[None, :]
    cos_t, sin_t = _rope_tables_t(s)

    dilations = [dil for _, dil in DILATED_PATTERNS]
    assert dilations[0] == 1
    grouped_dilations = dilations[1:]
    qt, k, vt, *qkv_b = _proj(x, norm1_g[None, :], wa, wb, ga, cos_t, sin_t, gqb, gkb, grouped_dilations)

    oat = _attn_a(bounded_a, rescale_a.reshape(1), qt, k, vt)
    bounded_b = (_score_bound(qb_norm_g, kb_norm_g) <= SAFE_SCORE_BOUND).astype(jnp.int32).reshape(1)
    results = [_dilated_rows(bounded_b, *qkv_b[:3])]
    for n, dil in enumerate(grouped_dilations):
        results.append(_dilated_grouped(bounded_b, *qkv_b[3 * (n + 1):3 * (n + 2)], dil))
    obs, lses = zip(*results)
    wo = w_out.astype(BF16)
    x2 = _merge(x, oat, obs, lses, outa_norm_g[None, :], outb_norm_g[None, :], wo[:WIDTH_A], wo[WIDTH_A:],
                grouped_dilations)
    return _ffn(x2, norm2_g[None, :], w_up.astype(BF16), conv_w, conv_b[None, :], w_down.astype(BF16))
```

```python
import functools
import math

import jax
import jax.numpy as jnp
import numpy as np
from jax import lax
from jax.experimental import pallas as pl
from jax.experimental.pallas import tpu as pltpu

F32 = jnp.float32
BF16 = jnp.bfloat16
FP8 = jnp.float8_e4m3fn

HEAD_DIM = 64
N_HEADS_A = 8
N_KV_A = 2
GQA_GROUP = N_HEADS_A // N_KV_A
N_HEADS_B = 8
WIDTH_A = N_HEADS_A * HEAD_DIM
WIDTH_B = N_HEADS_B * HEAD_DIM
KV_WIDTH_A = N_KV_A * HEAD_DIM
GRID_W = 64
ROPE_THETA = 10000.0
ROPE_AXIS_DIM = HEAD_DIM // 2
DIL_BLOCK = 64
DILATED_PATTERNS = ((128, 1), (512, 4), (2048, 16))
NORM_EPS = 1e-6
NEG_INF = -1e30

LANES = 128
SUBLANES = 8
MXU_DIM = 256
VMEM_LIMIT_BYTES = 56 * 1024 * 1024

PROJ_TM = 1024
GROUP_TILE = 512
GROUP_STEP = 4
ATT_TQ = 2048
ATT_TK = 2048
SAFE_SCORE_BOUND = 40.0
VT_ROWS = HEAD_DIM + 16
DIL_TL = 512
DIL_SUB = 128
DIL_UNROLL = 4
FFN_TM = 1024
FFN_FC = 256
FFN_HALO = 16
LSE_LANES = 16


def _nt_dot(a, b):
    return lax.dot_general(a, b, (((1,), (1,)), ((), ())), preferred_element_type=F32)


def _proj_kernel(x_ref, g1_ref, wa_ref, wb_ref, ga_ref, cos_ref, sin_ref, gqb_ref, gkb_ref,
                 qt_ref, k_ref, vt_ref, qb_ref, kb_ref, vb_ref, *rest):
    *grouped_refs, nat_all_ref, lvl_all_ref = rest
    for part in range(x_ref.shape[1] // GROUP_TILE):
        _proj_rows(part * GROUP_TILE, x_ref, g1_ref, wa_ref, wb_ref, ga_ref, cos_ref, sin_ref, gqb_ref, gkb_ref,
                   qt_ref, k_ref, vt_ref, qb_ref, kb_ref, vb_ref, grouped_refs,
                   nat_all_ref.at[part], lvl_all_ref.at[part])


def _proj_rows(r0, x_ref, g1_ref, wa_ref, wb_ref, ga_ref, cos_ref, sin_ref, gqb_ref, gkb_ref,
               qt_ref, k_ref, vt_ref, qb_ref, kb_ref, vb_ref, grouped_refs, nat_ref, lvl_ref):
    tm = GROUP_TILE
    rows = slice(r0, r0 + tm)
    x = x_ref[0, rows]
    ms = jnp.mean(x * x, axis=-1, keepdims=True)
    hn = (x * lax.rsqrt(ms + NORM_EPS) * g1_ref[...]).astype(BF16)
    ya = jnp.dot(hn, wa_ref[...], preferred_element_type=F32)
    yb = jnp.dot(hn, wb_ref[...], preferred_element_type=F32)

    ya_t = ya.T
    ones_rows = jnp.where(lax.broadcasted_iota(jnp.int32, (VT_ROWS - HEAD_DIM, tm), 0) == 0, 1.0, 0.0).astype(BF16)
    for g in range(N_KV_A):
        v_rows = ya_t[WIDTH_A + KV_WIDTH_A + HEAD_DIM * g:WIDTH_A + KV_WIDTH_A + HEAD_DIM * (g + 1)]
        vt_ref[0, g, :, rows] = jnp.concatenate([v_rows.astype(BF16), ones_rows], axis=0)
    cos_t = cos_ref[:, rows]
    sin_t = sin_ref[:, rows]
    reps = tm // LANES
    zeros = jnp.zeros((HEAD_DIM, tm), F32)
    q16 = ROPE_AXIS_DIM // 2

    def limbs8(val):
        hi = val.astype(FP8).astype(F32)
        return hi, (val - hi).astype(FP8).astype(F32)

    for h in range(N_HEADS_A + N_KV_A):
        u = ya_t[HEAD_DIM * h:HEAD_DIM * (h + 1)]
        r = lax.rsqrt(jnp.mean(u * u, axis=0, keepdims=True) + NORM_EPS)
        g = jnp.tile(ga_ref[HEAD_DIM * h:HEAD_DIM * (h + 1), :], (1, reps))
        un = u * r * g
        partner = jnp.concatenate(
            [un[q16:2 * q16], un[0:q16], un[3 * q16:4 * q16], un[2 * q16:3 * q16]], axis=0)
        hi, lo = limbs8(un * cos_t + partner * sin_t)
        if h < N_HEADS_A:
            qt_ref[0, h, :, rows] = jnp.concatenate([hi, lo, hi, zeros], axis=0).astype(FP8)
        else:
            k_ref[0, h - N_HEADS_A, rows] = jnp.concatenate([hi, hi, lo, zeros], axis=0).T.astype(FP8)

    ri = lax.broadcasted_iota(jnp.int32, (MXU_DIM, MXU_DIM), 0) // HEAD_DIM
    ci = lax.broadcasted_iota(jnp.int32, (MXU_DIM, MXU_DIM), 1) // HEAD_DIM
    ones_bd = jnp.where(ri == ci, 1.0, 0.0).astype(BF16)

    def head_norm(y, g_ref):
        sq = (y * y).astype(BF16)
        ss = jnp.concatenate(
            [jnp.dot(sq[:, c:c + MXU_DIM], ones_bd, preferred_element_type=F32)
             for c in range(0, y.shape[1], MXU_DIM)], axis=1)
        return y * lax.rsqrt(ss * (1.0 / HEAD_DIM) + NORM_EPS) * g_ref[...]

    parts = (head_norm(yb[:, :WIDTH_B], gqb_ref), head_norm(yb[:, WIDTH_B:2 * WIDTH_B], gkb_ref),
             yb[:, 2 * WIDTH_B:])
    slabs = WIDTH_B // LANES
    for c, (part, ref) in enumerate(zip(parts, (qb_ref, kb_ref, vb_ref))):
        ref[0, rows] = part.astype(BF16)
        for jj in range(slabs):
            nat_ref[slabs * c + jj] = part[:, LANES * jj:LANES * (jj + 1)]

    src_ref, block = nat_ref, tm
    n_levels = len(grouped_refs) // 3
    for n in range(n_levels):
        sub = block // GROUP_STEP
        for j in range(3 * slabs):
            c, jj = divmod(j, slabs)
            dst_ref = grouped_refs[3 * n + c]
            for lo in range(0, tm, sub):
                blk0, r = (lo // block) * block, (lo % block) // sub
                picked = src_ref[j, pl.ds(blk0 + r, sub, stride=GROUP_STEP), :]
                if n + 1 < n_levels:
                    lvl_ref[j, lo:lo + sub, :] = picked
                dst_ref[0, r0 + lo:r0 + lo + sub, LANES * jj:LANES * (jj + 1)] = picked.astype(BF16)
        src_ref, block = lvl_ref, sub


def _group_index(tau, tm, dilation):
    base, size, local = 0, tm, tau
    while dilation > 1:
        size //= GROUP_STEP
        base = base + (local % GROUP_STEP) * size
        local = local // GROUP_STEP
        dilation //= GROUP_STEP
    return base + local


def _proj(x, g1, wa, wb, ga, cos_t, sin_t, gqb, gkb, grouped_dilations):
    b, s, d = x.shape
    tm = PROJ_TM
    assert tuple(grouped_dilations) == tuple(GROUP_STEP ** (n + 1) for n in range(len(grouped_dilations)))
    assert len(grouped_dilations) <= 2
    full = lambda shape: pl.BlockSpec(shape, lambda bi, i: (0,) * len(shape))
    rows_b = jax.ShapeDtypeStruct((b, s, WIDTH_B), BF16)
    out_shape = (
        jax.ShapeDtypeStruct((b, N_HEADS_A, MXU_DIM, s), FP8),
        jax.ShapeDtypeStruct((b, N_KV_A, s, MXU_DIM), FP8),
        jax.ShapeDtypeStruct((b, N_KV_A, VT_ROWS, s), BF16),
    ) + (rows_b,) * (3 + 3 * len(grouped_dilations))
    row = lambda w: pl.BlockSpec((1, tm, w), lambda bi, i: (bi, i, 0))
    return pl.pallas_call(
        _proj_kernel,
        grid=(b, s // tm),
        in_specs=[
            row(d), full(g1.shape), full(wa.shape), full(wb.shape), full(ga.shape),
            pl.BlockSpec((HEAD_DIM, tm), lambda bi, i: (0, i)),
            pl.BlockSpec((HEAD_DIM, tm), lambda bi, i: (0, i)),
            full(gqb.shape), full(gkb.shape),
        ],
        out_specs=(
            pl.BlockSpec((1, N_HEADS_A, MXU_DIM, tm), lambda bi, i: (bi, 0, 0, i)),
            pl.BlockSpec((1, N_KV_A, tm, MXU_DIM), lambda bi, i: (bi, 0, i, 0)),
            pl.BlockSpec((1, N_KV_A, VT_ROWS, tm), lambda bi, i: (bi, 0, 0, i)),
        ) + (row(WIDTH_B),) * (3 + 3 * len(grouped_dilations)),
        out_shape=out_shape,
        scratch_shapes=[pltpu.VMEM((tm // GROUP_TILE, 3 * WIDTH_B // LANES, GROUP_TILE, LANES), F32)] * 2,
        compiler_params=pltpu.CompilerParams(
            dimension_semantics=("parallel", "parallel"), vmem_limit_bytes=VMEM_LIMIT_BYTES),
        name="proj",
    )(x, g1, wa, wb, ga, cos_t, sin_t, gqb, gkb)


def _attn_a_kernel(bounded_ref, rescale_ref, qt_ref, k_ref, vt_ref, ot_ref, m_ref, acc_ref):
    j = pl.program_id(3)
    bounded = bounded_ref[0] != 0

    @pl.when(j == 0)
    def _():
        m_ref[...] = jnp.full(m_ref.shape, NEG_INF, F32)
        acc_ref[...] = jnp.zeros(acc_ref.shape, F32)

    k = k_ref[0, 0]
    vt = vt_ref[0, 0]

    def scores(h):
        return jnp.dot(k, qt_ref[0, h], preferred_element_type=F32) * rescale_ref[0]

    @pl.when(bounded)
    def _():
        tk, tq = k_ref.shape[2], qt_ref.shape[3]
        for h in range(GQA_GROUP):
            for c in range(0, tq, MXU_DIM):
                q_c = qt_ref[0, h, :, c:c + MXU_DIM]
                part = None
                for r in range(0, tk, MXU_DIM):
                    s_t = jnp.dot(k_ref[0, 0, r:r + MXU_DIM, :], q_c, preferred_element_type=F32)
                    pv = jnp.dot(vt_ref[0, 0, :, r:r + MXU_DIM], jnp.exp2(s_t).astype(BF16),
                                 preferred_element_type=F32)
                    part = pv if part is None else part + pv
                acc_ref[h, :, c:c + MXU_DIM] += part

    @pl.when(jnp.logical_not(bounded))
    def _():
        for h in range(GQA_GROUP):
            s_t = scores(h)
            m_prev = m_ref[h]
            m_new = jnp.maximum(m_prev, jnp.max(s_t, axis=0, keepdims=True))
            p_t = jnp.exp2(s_t - m_new).astype(BF16)
            acc_ref[h] = (jnp.exp2(m_prev - m_new) * acc_ref[h]
                          + jnp.dot(vt, p_t, preferred_element_type=F32))
            m_ref[h] = m_new

    @pl.when(j == pl.num_programs(3) - 1)
    def _():
        for h in range(GQA_GROUP):
            a = acc_ref[h]
            ot_ref[0, HEAD_DIM * h:HEAD_DIM * (h + 1), :] = (
                a[:HEAD_DIM] * (1.0 / a[HEAD_DIM:HEAD_DIM + 1])).astype(BF16)


def _attn_a(bounded, rescale, qt, k, vt):
    b, _, _, s = qt.shape
    tq, tk = ATT_TQ, ATT_TK
    gw = GQA_GROUP * HEAD_DIM
    return pl.pallas_call(
        _attn_a_kernel,
        grid=(b, N_KV_A, s // tq, s // tk),
        in_specs=[
            pl.BlockSpec(memory_space=pltpu.SMEM),
            pl.BlockSpec(memory_space=pltpu.SMEM),
            pl.BlockSpec((1, GQA_GROUP, MXU_DIM, tq), lambda bi, g, i, j: (bi, g, 0, i)),
            pl.BlockSpec((1, 1, tk, MXU_DIM), lambda bi, g, i, j: (bi, g, j, 0)),
            pl.BlockSpec((1, 1, VT_ROWS, tk), lambda bi, g, i, j: (bi, g, 0, j)),
        ],
        out_specs=pl.BlockSpec((1, gw, tq), lambda bi, g, i, j: (bi, g, i)),
        out_shape=jax.ShapeDtypeStruct((b, WIDTH_A, s), BF16),
        scratch_shapes=[
            pltpu.VMEM((GQA_GROUP, 1, tq), F32),
            pltpu.VMEM((GQA_GROUP, VT_ROWS, tq), F32),
        ],
        compiler_params=pltpu.CompilerParams(
            dimension_semantics=("parallel", "parallel", "parallel", "arbitrary"),
            vmem_limit_bytes=VMEM_LIMIT_BYTES),
        name="attn_a",
    )(bounded, rescale, qt, k, vt)


def _dil_bias_init(bias_ref, dilation):
    tk = DIL_SUB + 2 * DIL_BLOCK
    qi = lax.broadcasted_iota(jnp.int32, (DIL_SUB, tk), 0)
    kj = lax.broadcasted_iota(jnp.int32, (DIL_SUB, tk), 1)
    aoff = jnp.abs(kj - DIL_BLOCK - qi)
    dist = (aoff * dilation).astype(F32)
    for h in range(N_HEADS_B):
        slope = 2.0 ** (-8.0 * (h + 1) / N_HEADS_B) * math.log2(math.e)
        bias_ref[h] = jnp.where(aoff <= DIL_BLOCK, -slope * dist, NEG_INF)


def _lse_lane_pair(lane):
    return (lane % HEAD_DIM) // LSE_LANES


def _lse_lane_of_head(h):
    return HEAD_DIM * (h % 2) + LSE_LANES * (h // 2)


def _dil_sub_tiles(bounded, q_ref, kext_ref, vext_ref, o_ref, lse_ref, bias_ref, *, n_sub, first_key, length):
    tk = DIL_SUB + 2 * DIL_BLOCK
    first_head = lax.broadcasted_iota(jnp.int32, (DIL_SUB, LANES), 1) < HEAD_DIM
    first_head_k = lax.broadcasted_iota(jnp.int32, (tk, LANES), 1) < HEAD_DIM
    lse_owner = _lse_lane_pair(lax.broadcasted_iota(jnp.int32, (DIL_SUB, LANES), 1))
    key_lane = lax.broadcasted_iota(jnp.int32, (1, tk), 1)
    stack_row = lax.broadcasted_iota(jnp.int32, (2 * tk, LANES), 0) < tk
    stack_lane = lax.broadcasted_iota(jnp.int32, (2 * tk, LANES), 1) < HEAD_DIM
    ones_stack = jnp.where(stack_row == stack_lane, 1.0, 0.0).astype(BF16)

    def sub_tile(st, carry, *, use_max):
        r0 = pl.multiple_of(st * DIL_SUB, DIL_SUB)
        kidx = first_key + r0 + key_lane
        in_range = (kidx >= 0) & (kidx < length)
        lse_tile = jnp.zeros((DIL_SUB, LANES), F32)
        for hp in range(N_HEADS_B // 2):
            cols = slice(LANES * hp, LANES * (hp + 1))
            qp = q_ref[pl.ds(r0, DIL_SUB), cols]
            kp = kext_ref[pl.ds(r0, tk), cols]
            vp = vext_ref[pl.ds(r0, tk), cols]
            zeros_v = jnp.zeros_like(vp)
            v_stack = jnp.concatenate([jnp.where(first_head_k, vp, zeros_v),
                                       jnp.where(first_head_k, zeros_v, vp)], axis=0)
            ps, ms = [], []
            for e in range(2):
                own = first_head if e == 0 else jnp.logical_not(first_head)
                qm = jnp.where(own, qp, jnp.zeros_like(qp))
                s = _nt_dot(qm, kp) + bias_ref[2 * hp + e]
                s = jnp.where(in_range, s, NEG_INF)
                if use_max:
                    ms.append(jnp.max(s, axis=-1, keepdims=True))
                    s = s - ms[-1]
                ps.append(jnp.exp2(s).astype(BF16))
            p_pair = jnp.concatenate(ps, axis=1)
            both = jnp.dot(p_pair, jnp.concatenate([v_stack, ones_stack], axis=1), preferred_element_type=F32)
            pv, l_pair = both[:, :LANES], both[:, LANES:]
            o_ref[pl.ds(r0, DIL_SUB), cols] = (pv * (1.0 / l_pair)).astype(BF16)
            lse = jnp.log(l_pair)
            if use_max:
                lse = lse + jnp.where(first_head, ms[0], ms[1]) * math.log(2.0)
            lse_tile = jnp.where(lse_owner == hp, lse, lse_tile)
        lse_ref[pl.ds(r0, DIL_SUB), :] = lse_tile
        return carry

    @pl.when(bounded)
    def _():
        lax.fori_loop(0, n_sub, functools.partial(sub_tile, use_max=False), 0, unroll=DIL_UNROLL)

    @pl.when(jnp.logical_not(bounded))
    def _():
        lax.fori_loop(0, n_sub, functools.partial(sub_tile, use_max=True), 0)


def _dil_rows_kernel(bounded_ref, q_ref, kp_ref, kc_ref, kn_ref, vp_ref, vc_ref, vn_ref, o_ref, lse_ref,
                     kext_ref, vext_ref, bias_ref, *, length):
    tl = q_ref.shape[1]
    bi, i = pl.program_id(0), pl.program_id(1)

    @pl.when((bi == 0) & (i == 0))
    def _():
        _dil_bias_init(bias_ref, 1)

    kext_ref[0:DIL_BLOCK] = kp_ref[0]
    kext_ref[DIL_BLOCK:DIL_BLOCK + tl] = kc_ref[0]
    kext_ref[DIL_BLOCK + tl:] = kn_ref[0]
    vext_ref[0:DIL_BLOCK] = vp_ref[0]
    vext_ref[DIL_BLOCK:DIL_BLOCK + tl] = vc_ref[0]
    vext_ref[DIL_BLOCK + tl:] = vn_ref[0]
    _dil_sub_tiles(bounded_ref[0] != 0, q_ref.at[0], kext_ref, vext_ref, o_ref.at[0], lse_ref.at[0], bias_ref,
                   n_sub=tl // DIL_SUB, first_key=i * tl - DIL_BLOCK, length=length)


def _dilated_rows(bounded, q, k, v):
    b, s, w = q.shape
    tl = DIL_TL
    blocks_per_tile = tl // DIL_BLOCK
    n_blocks = s // DIL_BLOCK
    cur = pl.BlockSpec((1, tl, w), lambda bi, i: (bi, i, 0))
    prev = pl.BlockSpec((1, DIL_BLOCK, w), lambda bi, i: (bi, jnp.maximum(i * blocks_per_tile - 1, 0), 0))
    nxt = pl.BlockSpec((1, DIL_BLOCK, w),
                       lambda bi, i: (bi, jnp.minimum((i + 1) * blocks_per_tile, n_blocks - 1), 0))
    return pl.pallas_call(
        functools.partial(_dil_rows_kernel, length=s),
        grid=(b, s // tl),
        in_specs=[pl.BlockSpec(memory_space=pltpu.SMEM), cur, prev, cur, nxt, prev, cur, nxt],
        out_specs=(cur, pl.BlockSpec((1, tl, LANES), lambda bi, i: (bi, i, 0))),
        out_shape=(jax.ShapeDtypeStruct((b, s, w), BF16), jax.ShapeDtypeStruct((b, s, LANES), F32)),
        scratch_shapes=[
            pltpu.VMEM((tl + 2 * DIL_BLOCK, w), BF16),
            pltpu.VMEM((tl + 2 * DIL_BLOCK, w), BF16),
            pltpu.VMEM((N_HEADS_B, DIL_SUB, DIL_SUB + 2 * DIL_BLOCK), F32),
        ],
        compiler_params=pltpu.CompilerParams(
            dimension_semantics=("arbitrary", "arbitrary"), vmem_limit_bytes=VMEM_LIMIT_BYTES),
        name="dil_1",
    )(bounded, q, k, k, k, v, v, v)


def _dil_grouped_kernel(bounded_ref, q_ref, k_ref, v_ref, o_ref, lse_ref,
                        qflat_ref, kext_ref, vext_ref, oflat_ref, lseflat_ref, bias_ref, *, dilation):
    n_tiles, rows = q_ref.shape[1], q_ref.shape[2]
    length = n_tiles * rows
    bi, r = pl.program_id(0), pl.program_id(1)

    @pl.when((bi == 0) & (r == 0))
    def _():
        _dil_bias_init(bias_ref, dilation)

    border = jnp.zeros((DIL_BLOCK, kext_ref.shape[1]), BF16)
    for ext_ref in (kext_ref, vext_ref):
        ext_ref[0:DIL_BLOCK] = border
        ext_ref[DIL_BLOCK + length:] = border
    for t in range(n_tiles):
        qflat_ref[rows * t:rows * (t + 1)] = q_ref[0, t]
        kext_ref[DIL_BLOCK + rows * t:DIL_BLOCK + rows * (t + 1)] = k_ref[0, t]
        vext_ref[DIL_BLOCK + rows * t:DIL_BLOCK + rows * (t + 1)] = v_ref[0, t]
    _dil_sub_tiles(bounded_ref[0] != 0, qflat_ref, kext_ref, vext_ref, oflat_ref, lseflat_ref, bias_ref,
                   n_sub=length // DIL_SUB, first_key=-DIL_BLOCK, length=length)
    for t in range(n_tiles):
        o_ref[0, t] = oflat_ref[rows * t:rows * (t + 1)]
        lse_ref[0, t] = lseflat_ref[rows * t:rows * (t + 1)]


def _dilated_grouped(bounded, q, k, v, dilation):
    b, s, w = q.shape
    n_tiles, rows = s // GROUP_TILE, GROUP_TILE // dilation
    length = n_tiles * rows
    view = lambda a: a.reshape(b, n_tiles, dilation, rows, a.shape[-1])
    spec = lambda width: pl.BlockSpec((1, n_tiles, None, rows, width), lambda bi, r: (bi, 0, r, 0, 0))
    o, lse = pl.pallas_call(
        functools.partial(_dil_grouped_kernel, dilation=dilation),
        grid=(b, dilation),
        in_specs=[pl.BlockSpec(memory_space=pltpu.SMEM), spec(w), spec(w), spec(w)],
        out_specs=(spec(w), spec(LANES)),
        out_shape=(jax.ShapeDtypeStruct((b, n_tiles, dilation, rows, w), BF16),
                   jax.ShapeDtypeStruct((b, n_tiles, dilation, rows, LANES), F32)),
        scratch_shapes=[
            pltpu.VMEM((length, w), BF16),
            pltpu.VMEM((length + 2 * DIL_BLOCK, w), BF16),
            pltpu.VMEM((length + 2 * DIL_BLOCK, w), BF16),
            pltpu.VMEM((length, w), BF16),
            pltpu.VMEM((length, LANES), F32),
            pltpu.VMEM((N_HEADS_B, DIL_SUB, DIL_SUB + 2 * DIL_BLOCK), F32),
        ],
        compiler_params=pltpu.CompilerParams(
            dimension_semantics=("arbitrary", "arbitrary"), vmem_limit_bytes=VMEM_LIMIT_BYTES),
        name=f"dil_{dilation}",
    )(bounded, view(q), view(k), view(v))
    return o.reshape(b, s, w), lse.reshape(b, s, LANES)


def _ungroup_rows(val, levels, scratch_refs):
    slabs = val.shape[1] // LANES
    for lvl in reversed(range(levels)):
        block = GROUP_TILE // GROUP_STEP ** lvl
        sub = block // GROUP_STEP
        ref = scratch_refs[lvl]
        for j in range(slabs):
            for lo in range(0, GROUP_TILE, sub):
                blk0, r = (lo // block) * block, (lo % block) // sub
                ref[j, pl.ds(blk0 + r, sub, stride=GROUP_STEP), :] = val[lo:lo + sub, LANES * j:LANES * (j + 1)]
        val = jnp.concatenate([ref[j] for j in range(slabs)], axis=1)
    return val


def _merge_kernel(x_ref, oat_ref, o1_ref, o2_ref, o3_ref, l1_ref, l2_ref, l3_ref,
                  ga_ref, gb_ref, wo_ref, out_ref, *scratch):
    def group_norm(y, g_ref):
        ms = jnp.mean(y * y, axis=-1, keepdims=True)
        return (y * lax.rsqrt(ms + NORM_EPS) * g_ref[...]).astype(BF16)

    oa = oat_ref[0].astype(F32).T

    o2_scr, o3_scr, l2_scr, l3_scr = scratch[0:1], scratch[1:3], scratch[3:4], scratch[4:6]

    def limbs(val):
        hi = val.astype(BF16)
        return hi, (val - hi.astype(F32)).astype(BF16)

    l1, l2, l3 = l1_ref[0], _ungroup_rows(l2_ref[0], 1, l2_scr), _ungroup_rows(l3_ref[0], 2, l3_scr)
    mx = jnp.maximum(jnp.maximum(l1, l2), l3)
    e1, e2, e3 = jnp.exp(l1 - mx), jnp.exp(l2 - mx), jnp.exp(l3 - mx)
    inv = 1.0 / (e1 + e2 + e3)
    ri = lax.broadcasted_iota(jnp.int32, (2 * LANES, WIDTH_B), 0) % LANES
    ci = lax.broadcasted_iota(jnp.int32, (2 * LANES, WIDTH_B), 1)
    expand = jnp.where(ri == _lse_lane_of_head(ci // HEAD_DIM), 1.0, 0.0).astype(BF16)

    def widen(wgt):
        return jnp.dot(jnp.concatenate(limbs(wgt), axis=1), expand, preferred_element_type=F32)

    ob = (widen(e1 * inv) * o1_ref[0].astype(F32)
          + widen(e2 * inv) * _ungroup_rows(o2_ref[0].astype(F32), 1, o2_scr)
          + widen(e3 * inv) * _ungroup_rows(o3_ref[0].astype(F32), 2, o3_scr))
    mixed_a = group_norm(oa, ga_ref)
    mixed_b = group_norm(ob, gb_ref)
    out_ref[0] = (x_ref[0]
                  + jnp.dot(mixed_a, wo_ref[:WIDTH_A], preferred_element_type=F32)
                  + jnp.dot(mixed_b, wo_ref[WIDTH_A:], preferred_element_type=F32))


def _merge(x, oat, obs, lses, ga, gb, wo, grouped_dilations):
    b, s, d = x.shape
    tm = GROUP_TILE
    assert tuple(grouped_dilations) == (GROUP_STEP, GROUP_STEP ** 2)
    slab = lambda n: pltpu.VMEM((n, tm, LANES), F32)
    wide, narrow = WIDTH_B // LANES, 1
    full = lambda shape: pl.BlockSpec(shape, lambda bi, i: (0,) * len(shape))
    row = lambda w: pl.BlockSpec((1, tm, w), lambda bi, i: (bi, i, 0))
    return pl.pallas_call(
        _merge_kernel,
        grid=(b, s // tm),
        in_specs=[row(d), pl.BlockSpec((1, WIDTH_A, tm), lambda bi, i: (bi, 0, i)),
                  row(WIDTH_B), row(WIDTH_B), row(WIDTH_B), row(LANES), row(LANES), row(LANES),
                  full(ga.shape), full(gb.shape), full(wo.shape)],
        out_specs=row(d),
        out_shape=jax.ShapeDtypeStruct((b, s, d), F32),
        scratch_shapes=[slab(wide), slab(wide), slab(wide), slab(narrow), slab(narrow), slab(narrow)],
        compiler_params=pltpu.CompilerParams(
            dimension_semantics=("parallel", "parallel"), vmem_limit_bytes=VMEM_LIMIT_BYTES),
        name="merge",
    )(x, oat, *obs, *lses, ga, gb, wo)


def _gelu_tanh(x):
    c = math.sqrt(2.0 / math.pi)
    return 0.5 * x * (1.0 + jnp.tanh(c * (x + 0.044715 * (x * x * x))))


def _ffn_kernel(xp_ref, xc_ref, xn_ref, g2_ref, wup_ref, cw_ref, cb_ref, wdn_ref, out_ref,
                hext_ref, act_ref, *, d_ff):
    tm = xc_ref.shape[1]
    halo = xp_ref.shape[1]
    i = pl.program_id(1)
    rows = tm + 2 * halo

    def normed(x):
        ms = jnp.mean(x * x, axis=-1, keepdims=True)
        return x * lax.rsqrt(ms + NORM_EPS) * g2_ref[...]

    keep_prev = (i > 0).astype(F32)
    keep_next = (i < pl.num_programs(1) - 1).astype(F32)
    hext_ref[0:halo] = (normed(xp_ref[0]) * keep_prev).astype(BF16)
    hext_ref[halo:halo + tm] = normed(xc_ref[0]).astype(BF16)
    hext_ref[halo + tm:] = (normed(xn_ref[0]) * keep_next).astype(BF16)
    hext = hext_ref[...]

    def conv(u, col):
        w = cw_ref[:, col:col + FFN_FC]
        up = pltpu.roll(u, 1, 0)
        dn = pltpu.roll(u, rows - 1, 0)
        y = up * w[0:1] + u * w[1:2] + dn * w[2:3] + cb_ref[:, col:col + FFN_FC]
        return y[halo:halo + tm]

    for c in range(0, d_ff, FFN_FC):
        ug = jnp.dot(hext, wup_ref[:, c:c + FFN_FC], preferred_element_type=F32)
        uv = jnp.dot(hext, wup_ref[:, d_ff + c:d_ff + c + FFN_FC], preferred_element_type=F32)
        act_ref[:, c:c + FFN_FC] = (_gelu_tanh(conv(ug, c)) * conv(uv, d_ff + c)).astype(BF16)

    out_ref[0] = xc_ref[0] + jnp.dot(act_ref[...], wdn_ref[...], preferred_element_type=F32)


def _ffn(x, g2, w_up, conv_w, conv_b, w_down):
    b, s, d = x.shape
    d_ff = w_down.shape[0]
    tm, halo = FFN_TM, FFN_HALO
    per_tile = tm // halo
    n_halo_blocks = s // halo
    full = lambda shape: pl.BlockSpec(shape, lambda bi, i: (0,) * len(shape))
    resident = lambda shape: pl.BlockSpec(shape, lambda bi, i: (0,) * len(shape), pipeline_mode=pl.Buffered(1))
    return pl.pallas_call(
        functools.partial(_ffn_kernel, d_ff=d_ff),
        grid=(b, s // tm),
        in_specs=[
            pl.BlockSpec((1, halo, d), lambda bi, i: (bi, jnp.maximum(i * per_tile - 1, 0), 0)),
            pl.BlockSpec((1, tm, d), lambda bi, i: (bi, i, 0)),
            pl.BlockSpec((1, halo, d), lambda bi, i: (bi, jnp.minimum((i + 1) * per_tile, n_halo_blocks - 1), 0)),
            full(g2.shape), resident(w_up.shape), full(conv_w.shape), full(conv_b.shape), resident(w_down.shape),
        ],
        out_specs=pl.BlockSpec((1, tm, d), lambda bi, i: (bi, i, 0)),
        out_shape=jax.ShapeDtypeStruct((b, s, d), F32),
        scratch_shapes=[pltpu.VMEM((tm + 2 * halo, d), BF16), pltpu.VMEM((tm, d_ff), BF16)],
        compiler_params=pltpu.CompilerParams(
            dimension_semantics=("parallel", "parallel"), vmem_limit_bytes=VMEM_LIMIT_BYTES),
        name="ffn",
    )(x, x, x, g2, w_up, conv_w, conv_b, w_down)


def _rope_tables_t(seq_len):
    rows = seq_len // GRID_W
    row = np.repeat(np.arange(rows, dtype=np.float32), GRID_W)
    col = np.tile(np.arange(GRID_W, dtype=np.float32), rows)
    inv = (np.float32(ROPE_THETA) ** (-np.arange(0, ROPE_AXIS_DIM, 2, dtype=np.float32) / np.float32(ROPE_AXIS_DIM))
           ).astype(np.float32)
    ang_r = (row[:, None] * inv[None, :]).T
    ang_c = (col[:, None] * inv[None, :]).T
    cos_t = np.concatenate([np.cos(ang_r)] * 2 + [np.cos(ang_c)] * 2, axis=0).astype(np.float32)
    sin_t = np.concatenate([-np.sin(ang_r), np.sin(ang_r), -np.sin(ang_c), np.sin(ang_c)], axis=0).astype(np.float32)
    return jnp.asarray(cos_t), jnp.asarray(sin_t)


def _score_bound(gq, gk):
    return math.sqrt(HEAD_DIM) * jnp.max(jnp.abs(gq)) * jnp.max(jnp.abs(gk))


def kernel(x, norm1_g, w_in, qa_norm_g, ka_norm_g, qb_norm_g, kb_norm_g, outa_norm_g, outb_norm_g,
           w_out, norm2_g, w_up, conv_w, conv_b, w_down):
    b, s, d = x.shape
    scale = HEAD_DIM ** -0.5 * math.log2(math.e)
    a_cols = WIDTH_A + 2 * KV_WIDTH_A
    wa = w_in[:, :a_cols].astype(BF16)
    wb = w_in[:, a_cols:].astype(BF16)
    bound_a = _score_bound(qa_norm_g, ka_norm_g)
    bounded_a = (bound_a <= SAFE_SCORE_BOUND).astype(jnp.int32).reshape(1)
    rescale_a = jnp.maximum(bound_a / SAFE_SCORE_BOUND, 1.0)
    positive = lambda g: jnp.where(g > 0, g, 1.0)
    gq_max, gk_max = positive(jnp.max(jnp.abs(qa_norm_g))), positive(jnp.max(jnp.abs(ka_norm_g)))
    ga = jnp.concatenate([jnp.tile(qa_norm_g, N_HEADS_A) * jnp.sqrt(scale * gk_max / (gq_max * rescale_a)),
                          jnp.tile(ka_norm_g, N_KV_A) * jnp.sqrt(scale * gq_max / (gk_max * rescale_a))])
    ga = jnp.broadcast_to(ga[:, None], (ga.shape[0], LANES))
    gqb = (jnp.tile(qb_norm_g, N_HEADS_B) * scale)[None, :]
    gkb = jnp.tile(kb_norm_g, N_HEADS_B)[None, :]
    cos_t, sin_t = _rope_tables_t(s)

    dilations = [dil for _, dil in DILATED_PATTERNS]
    assert dilations[0] == 1
    grouped_dilations = dilations[1:]
    qt, k, vt, *qkv_b = _proj(x, norm1_g[None, :], wa, wb, ga, cos_t, sin_t, gqb, gkb, grouped_dilations)

    oat = _attn_a(bounded_a, rescale_a.reshape(1), qt, k, vt)
    bounded_b = (_score_bound(qb_norm_g, kb_norm_g) <= SAFE_SCORE_BOUND).astype(jnp.int32).reshape(1)
    results = [_dilated_rows(bounded_b, *qkv_b[:3])]
    for n, dil in enumerate(grouped_dilations):
        results.append(_dilated_grouped(bounded_b, *qkv_b[3 * (n + 1):3 * (n + 2)], dil))
    obs, lses = zip(*results)
    x2 = _merge(x, oat, obs, lses, outa_norm_g[None, :], outb_norm_g[None, :], w_out.astype(BF16),
                grouped_dilations)
    return _ffn(x2, norm2_g[None, :], w_up.astype(BF16), conv_w, conv_b[None, :], w_down.astype(BF16))
```

```python
import functools
import math

import jax
import jax.numpy as jnp
import numpy as np
from jax import lax
from jax.experimental import pallas as pl
from jax.experimental.pallas import tpu as pltpu

F32 = jnp.float32
BF16 = jnp.bfloat16
FP8 = jnp.float8_e4m3fn

HEAD_DIM = 64
N_HEADS_A = 8
N_KV_A = 2
GQA_GROUP = N_HEADS_A // N_KV_A
N_HEADS_B = 8
WIDTH_A = N_HEADS_A * HEAD_DIM
WIDTH_B = N_HEADS_B * HEAD_DIM
KV_WIDTH_A = N_KV_A * HEAD_DIM
GRID_W = 64
ROPE_THETA = 10000.0
ROPE_AXIS_DIM = HEAD_DIM // 2
DIL_BLOCK = 64
DILATED_PATTERNS = ((128, 1), (512, 4), (2048, 16))
NORM_EPS = 1e-6
NEG_INF = -1e30

LANES = 128
MXU_DIM = 256
VMEM_LIMIT_BYTES = 56 * 1024 * 1024

PROJ_TM = 1024
GROUP_TILE = 512
GROUP_STEP = 4
ATT_TQ = 2048
ATT_TK = 2048
SAFE_SCORE_BOUND = 40.0
VT_ROWS = HEAD_DIM + 16
DIL_TL = 1024
DIL_SUB = 128
DIL_UNROLL = 4
FFN_TM = 1024
FFN_FC = 256
FFN_HALO = 16
LSE_LANES = 16


def _nt_dot(a, b):
    return lax.dot_general(a, b, (((1,), (1,)), ((), ())), preferred_element_type=F32)


def _proj_kernel(x_ref, g1_ref, wa_ref, wb_ref, ga_ref, cos_ref, sin_ref, gqb_ref, gkb_ref,
                 qt_ref, k_ref, vt_ref, qb_ref, kb_ref, vb_ref, *rest):
    *grouped_refs, nat_all_ref, lvl_all_ref = rest
    for part in range(x_ref.shape[1] // GROUP_TILE):
        _proj_rows(part * GROUP_TILE, x_ref, g1_ref, wa_ref, wb_ref, ga_ref, cos_ref, sin_ref, gqb_ref, gkb_ref,
                   qt_ref, k_ref, vt_ref, qb_ref, kb_ref, vb_ref, grouped_refs,
                   nat_all_ref.at[part], lvl_all_ref.at[part])


def _proj_rows(r0, x_ref, g1_ref, wa_ref, wb_ref, ga_ref, cos_ref, sin_ref, gqb_ref, gkb_ref,
               qt_ref, k_ref, vt_ref, qb_ref, kb_ref, vb_ref, grouped_refs, nat_ref, lvl_ref):
    tm = GROUP_TILE
    rows = slice(r0, r0 + tm)
    x = x_ref[0, rows]
    ms = jnp.mean(x * x, axis=-1, keepdims=True)
    hn = (x * lax.rsqrt(ms + NORM_EPS) * g1_ref[...]).astype(BF16)
    ya = jnp.dot(hn, wa_ref[...], preferred_element_type=F32)
    yb = jnp.dot(hn, wb_ref[...], preferred_element_type=F32)

    ya_t = ya.T
    ones_rows = jnp.where(lax.broadcasted_iota(jnp.int32, (VT_ROWS - HEAD_DIM, tm), 0) == 0, 1.0, 0.0).astype(BF16)
    for g in range(N_KV_A):
        v_rows = ya_t[WIDTH_A + KV_WIDTH_A + HEAD_DIM * g:WIDTH_A + KV_WIDTH_A + HEAD_DIM * (g + 1)]
        vt_ref[0, g, :, rows] = jnp.concatenate([v_rows.astype(BF16), ones_rows], axis=0)
    cos_t = cos_ref[:, rows]
    sin_t = sin_ref[:, rows]
    reps = tm // LANES
    zeros = jnp.zeros((HEAD_DIM, tm), F32)
    q16 = ROPE_AXIS_DIM // 2

    def limbs8(val):
        hi = val.astype(FP8).astype(F32)
        return hi, (val - hi).astype(FP8).astype(F32)

    for h in range(N_HEADS_A + N_KV_A):
        u = ya_t[HEAD_DIM * h:HEAD_DIM * (h + 1)]
        r = lax.rsqrt(jnp.mean(u * u, axis=0, keepdims=True) + NORM_EPS)
        g = jnp.tile(ga_ref[HEAD_DIM * h:HEAD_DIM * (h + 1), :], (1, reps))
        un = u * r * g
        partner = jnp.concatenate(
            [un[q16:2 * q16], un[0:q16], un[3 * q16:4 * q16], un[2 * q16:3 * q16]], axis=0)
        hi, lo = limbs8(un * cos_t + partner * sin_t)
        if h < N_HEADS_A:
            qt_ref[0, h, :, rows] = jnp.concatenate([hi, lo, hi, zeros], axis=0).astype(FP8)
        else:
            k_ref[0, h - N_HEADS_A, rows] = jnp.concatenate([hi, hi, lo, zeros], axis=0).T.astype(FP8)

    ri = lax.broadcasted_iota(jnp.int32, (MXU_DIM, MXU_DIM), 0) // HEAD_DIM
    ci = lax.broadcasted_iota(jnp.int32, (MXU_DIM, MXU_DIM), 1) // HEAD_DIM
    ones_bd = jnp.where(ri == ci, 1.0, 0.0).astype(BF16)

    def head_norm(y, g_ref):
        sq = (y * y).astype(BF16)
        ss = jnp.concatenate(
            [jnp.dot(sq[:, c:c + MXU_DIM], ones_bd, preferred_element_type=F32)
             for c in range(0, y.shape[1], MXU_DIM)], axis=1)
        return y * lax.rsqrt(ss * (1.0 / HEAD_DIM) + NORM_EPS) * g_ref[...]

    parts = (head_norm(yb[:, :WIDTH_B], gqb_ref), head_norm(yb[:, WIDTH_B:2 * WIDTH_B], gkb_ref),
             yb[:, 2 * WIDTH_B:])
    slabs = WIDTH_B // LANES
    for c, (part, ref) in enumerate(zip(parts, (qb_ref, kb_ref, vb_ref))):
        ref[0, rows] = part.astype(BF16)
        for jj in range(slabs):
            nat_ref[slabs * c + jj] = part[:, LANES * jj:LANES * (jj + 1)]

    src_ref, block = nat_ref, tm
    n_levels = len(grouped_refs) // 3
    for n in range(n_levels):
        sub = block // GROUP_STEP
        for j in range(3 * slabs):
            c, jj = divmod(j, slabs)
            dst_ref = grouped_refs[3 * n + c]
            for lo in range(0, tm, sub):
                blk0, r = (lo // block) * block, (lo % block) // sub
                picked = src_ref[j, pl.ds(blk0 + r, sub, stride=GROUP_STEP), :]
                if n + 1 < n_levels:
                    lvl_ref[j, lo:lo + sub, :] = picked
                dst_ref[0, r0 + lo:r0 + lo + sub, LANES * jj:LANES * (jj + 1)] = picked.astype(BF16)
        src_ref, block = lvl_ref, sub


def _group_index(tau, tm, dilation):
    base, size, local = 0, tm, tau
    while dilation > 1:
        size //= GROUP_STEP
        base = base + (local % GROUP_STEP) * size
        local = local // GROUP_STEP
        dilation //= GROUP_STEP
    return base + local


def _proj(x, g1, wa, wb, ga, cos_t, sin_t, gqb, gkb, grouped_dilations):
    b, s, d = x.shape
    tm = PROJ_TM
    assert tuple(grouped_dilations) == tuple(GROUP_STEP ** (n + 1) for n in range(len(grouped_dilations)))
    assert len(grouped_dilations) <= 2
    full = lambda shape: pl.BlockSpec(shape, lambda bi, i: (0,) * len(shape))
    rows_b = jax.ShapeDtypeStruct((b, s, WIDTH_B), BF16)
    out_shape = (
        jax.ShapeDtypeStruct((b, N_HEADS_A, MXU_DIM, s), FP8),
        jax.ShapeDtypeStruct((b, N_KV_A, s, MXU_DIM), FP8),
        jax.ShapeDtypeStruct((b, N_KV_A, VT_ROWS, s), BF16),
    ) + (rows_b,) * (3 + 3 * len(grouped_dilations))
    row = lambda w: pl.BlockSpec((1, tm, w), lambda bi, i: (bi, i, 0))
    return pl.pallas_call(
        _proj_kernel,
        grid=(b, s // tm),
        in_specs=[
            row(d), full(g1.shape), full(wa.shape), full(wb.shape), full(ga.shape),
            pl.BlockSpec((HEAD_DIM, tm), lambda bi, i: (0, i)),
            pl.BlockSpec((HEAD_DIM, tm), lambda bi, i: (0, i)),
            full(gqb.shape), full(gkb.shape),
        ],
        out_specs=(
            pl.BlockSpec((1, N_HEADS_A, MXU_DIM, tm), lambda bi, i: (bi, 0, 0, i)),
            pl.BlockSpec((1, N_KV_A, tm, MXU_DIM), lambda bi, i: (bi, 0, i, 0)),
            pl.BlockSpec((1, N_KV_A, VT_ROWS, tm), lambda bi, i: (bi, 0, 0, i)),
        ) + (row(WIDTH_B),) * (3 + 3 * len(grouped_dilations)),
        out_shape=out_shape,
        scratch_shapes=[pltpu.VMEM((tm // GROUP_TILE, 3 * WIDTH_B // LANES, GROUP_TILE, LANES), F32)] * 2,
        compiler_params=pltpu.CompilerParams(
            dimension_semantics=("parallel", "parallel"), vmem_limit_bytes=VMEM_LIMIT_BYTES),
        name="proj",
    )(x, g1, wa, wb, ga, cos_t, sin_t, gqb, gkb)


def _attn_a_kernel(bounded_ref, rescale_ref, qt_ref, k_ref, vt_ref, ot_ref, m_ref, acc_ref):
    j = pl.program_id(3)
    bounded = bounded_ref[0] != 0

    @pl.when(j == 0)
    def _():
        m_ref[...] = jnp.full(m_ref.shape, NEG_INF, F32)
        acc_ref[...] = jnp.zeros(acc_ref.shape, F32)

    k = k_ref[0, 0]
    vt = vt_ref[0, 0]

    def scores(h):
        return jnp.dot(k, qt_ref[0, h], preferred_element_type=F32) * rescale_ref[0]

    @pl.when(bounded)
    def _():
        tk, tq = k_ref.shape[2], qt_ref.shape[3]
        for h in range(GQA_GROUP):
            for c in range(0, tq, MXU_DIM):
                q_c = qt_ref[0, h, :, c:c + MXU_DIM]
                part = None
                for r in range(0, tk, MXU_DIM):
                    s_t = jnp.dot(k_ref[0, 0, r:r + MXU_DIM, :], q_c, preferred_element_type=F32)
                    pv = jnp.dot(vt_ref[0, 0, :, r:r + MXU_DIM], jnp.exp2(s_t).astype(BF16),
                                 preferred_element_type=F32)
                    part = pv if part is None else part + pv
                acc_ref[h, :, c:c + MXU_DIM] += part

    @pl.when(jnp.logical_not(bounded))
    def _():
        for h in range(GQA_GROUP):
            s_t = scores(h)
            m_prev = m_ref[h]
            m_new = jnp.maximum(m_prev, jnp.max(s_t, axis=0, keepdims=True))
            p_t = jnp.exp2(s_t - m_new).astype(BF16)
            acc_ref[h] = (jnp.exp2(m_prev - m_new) * acc_ref[h]
                          + jnp.dot(vt, p_t, preferred_element_type=F32))
            m_ref[h] = m_new

    @pl.when(j == pl.num_programs(3) - 1)
    def _():
        for h in range(GQA_GROUP):
            a = acc_ref[h]
            ot_ref[0, HEAD_DIM * h:HEAD_DIM * (h + 1), :] = (
                a[:HEAD_DIM] * (1.0 / a[HEAD_DIM:HEAD_DIM + 1])).astype(BF16)


def _attn_a(bounded, rescale, qt, k, vt):
    b, _, _, s = qt.shape
    tq, tk = ATT_TQ, ATT_TK
    gw = GQA_GROUP * HEAD_DIM
    return pl.pallas_call(
        _attn_a_kernel,
        grid=(b, N_KV_A, s // tq, s // tk),
        in_specs=[
            pl.BlockSpec(memory_space=pltpu.SMEM),
            pl.BlockSpec(memory_space=pltpu.SMEM),
            pl.BlockSpec((1, GQA_GROUP, MXU_DIM, tq), lambda bi, g, i, j: (bi, g, 0, i)),
            pl.BlockSpec((1, 1, tk, MXU_DIM), lambda bi, g, i, j: (bi, g, j, 0)),
            pl.BlockSpec((1, 1, VT_ROWS, tk), lambda bi, g, i, j: (bi, g, 0, j)),
        ],
        out_specs=pl.BlockSpec((1, gw, tq), lambda bi, g, i, j: (bi, g, i)),
        out_shape=jax.ShapeDtypeStruct((b, WIDTH_A, s), BF16),
        scratch_shapes=[
            pltpu.VMEM((GQA_GROUP, 1, tq), F32),
            pltpu.VMEM((GQA_GROUP, VT_ROWS, tq), F32),
        ],
        compiler_params=pltpu.CompilerParams(
            dimension_semantics=("parallel", "parallel", "parallel", "arbitrary"),
            vmem_limit_bytes=VMEM_LIMIT_BYTES),
        name="attn_a",
    )(bounded, rescale, qt, k, vt)


def _dil_bias_init(bias_ref, dilation):
    tk = DIL_SUB + 2 * DIL_BLOCK
    qi = lax.broadcasted_iota(jnp.int32, (DIL_SUB, tk), 0)
    kj = lax.broadcasted_iota(jnp.int32, (DIL_SUB, tk), 1)
    aoff = jnp.abs(kj - DIL_BLOCK - qi)
    dist = (aoff * dilation).astype(F32)
    for h in range(N_HEADS_B):
        slope = 2.0 ** (-8.0 * (h + 1) / N_HEADS_B) * math.log2(math.e)
        bias_ref[h] = jnp.where(aoff <= DIL_BLOCK, -slope * dist, NEG_INF)


def _lse_lane_pair(lane):
    return (lane % HEAD_DIM) // LSE_LANES


def _lse_lane_of_head(h):
    return HEAD_DIM * (h % 2) + LSE_LANES * (h // 2)


def _dil_sub_tiles(bounded, q_ref, kext_ref, vext_ref, o_ref, lse_ref, bias_ref, *, n_sub, first_key, length):
    tk = DIL_SUB + 2 * DIL_BLOCK
    first_head = lax.broadcasted_iota(jnp.int32, (DIL_SUB, LANES), 1) < HEAD_DIM
    first_head_k = lax.broadcasted_iota(jnp.int32, (tk, LANES), 1) < HEAD_DIM
    lse_owner = _lse_lane_pair(lax.broadcasted_iota(jnp.int32, (DIL_SUB, LANES), 1))
    key_lane = lax.broadcasted_iota(jnp.int32, (1, tk), 1)
    stack_row = lax.broadcasted_iota(jnp.int32, (2 * tk, LANES), 0) < tk
    stack_lane = lax.broadcasted_iota(jnp.int32, (2 * tk, LANES), 1) < HEAD_DIM
    ones_stack = jnp.where(stack_row == stack_lane, 1.0, 0.0).astype(BF16)

    def sub_tile(st, carry, *, use_max):
        r0 = pl.multiple_of(st * DIL_SUB, DIL_SUB)
        kidx = first_key + r0 + key_lane
        in_range = (kidx >= 0) & (kidx < length)
        lse_tile = jnp.zeros((DIL_SUB, LANES), F32)
        for hp in range(N_HEADS_B // 2):
            cols = slice(LANES * hp, LANES * (hp + 1))
            qp = q_ref[pl.ds(r0, DIL_SUB), cols]
            kp = kext_ref[pl.ds(r0, tk), cols]
            vp = vext_ref[pl.ds(r0, tk), cols]
            zeros_v = jnp.zeros_like(vp)
            v_stack = jnp.concatenate([jnp.where(first_head_k, vp, zeros_v),
                                       jnp.where(first_head_k, zeros_v, vp)], axis=0)
            ps, ms = [], []
            for e in range(2):
                own = first_head if e == 0 else jnp.logical_not(first_head)
                qm = jnp.where(own, qp, jnp.zeros_like(qp))
                s = _nt_dot(qm, kp) + bias_ref[2 * hp + e]
                s = jnp.where(in_range, s, NEG_INF)
                if use_max:
                    ms.append(jnp.max(s, axis=-1, keepdims=True))
                    s = s - ms[-1]
                ps.append(jnp.exp2(s).astype(BF16))
            p_pair = jnp.concatenate(ps, axis=1)
            both = jnp.dot(p_pair, jnp.concatenate([v_stack, ones_stack], axis=1), preferred_element_type=F32)
            pv, l_pair = both[:, :LANES], both[:, LANES:]
            o_ref[pl.ds(r0, DIL_SUB), cols] = (pv * (1.0 / l_pair)).astype(BF16)
            lse = jnp.log(l_pair)
            if use_max:
                lse = lse + jnp.where(first_head, ms[0], ms[1]) * math.log(2.0)
            lse_tile = jnp.where(lse_owner == hp, lse, lse_tile)
        lse_ref[pl.ds(r0, DIL_SUB), :] = lse_tile
        return carry

    @pl.when(bounded)
    def _():
        lax.fori_loop(0, n_sub, functools.partial(sub_tile, use_max=False), 0, unroll=DIL_UNROLL)

    @pl.when(jnp.logical_not(bounded))
    def _():
        lax.fori_loop(0, n_sub, functools.partial(sub_tile, use_max=True), 0)


def _dil_rows_kernel(bounded_ref, q_ref, kp_ref, kc_ref, kn_ref, vp_ref, vc_ref, vn_ref, o_ref, lse_ref,
                     kext_ref, vext_ref, bias_ref, *, length):
    tl = q_ref.shape[1]
    bi, i = pl.program_id(0), pl.program_id(1)

    @pl.when((bi == 0) & (i == 0))
    def _():
        _dil_bias_init(bias_ref, 1)

    kext_ref[0:DIL_BLOCK] = kp_ref[0]
    kext_ref[DIL_BLOCK:DIL_BLOCK + tl] = kc_ref[0]
    kext_ref[DIL_BLOCK + tl:] = kn_ref[0]
    vext_ref[0:DIL_BLOCK] = vp_ref[0]
    vext_ref[DIL_BLOCK:DIL_BLOCK + tl] = vc_ref[0]
    vext_ref[DIL_BLOCK + tl:] = vn_ref[0]
    _dil_sub_tiles(bounded_ref[0] != 0, q_ref.at[0], kext_ref, vext_ref, o_ref.at[0], lse_ref.at[0], bias_ref,
                   n_sub=tl // DIL_SUB, first_key=i * tl - DIL_BLOCK, length=length)


def _dilated_rows(bounded, q, k, v):
    b, s, w = q.shape
    tl = DIL_TL
    blocks_per_tile = tl // DIL_BLOCK
    n_blocks = s // DIL_BLOCK
    cur = pl.BlockSpec((1, tl, w), lambda bi, i: (bi, i, 0))
    prev = pl.BlockSpec((1, DIL_BLOCK, w), lambda bi, i: (bi, jnp.maximum(i * blocks_per_tile - 1, 0), 0))
    nxt = pl.BlockSpec((1, DIL_BLOCK, w),
                       lambda bi, i: (bi, jnp.minimum((i + 1) * blocks_per_tile, n_blocks - 1), 0))
    return pl.pallas_call(
        functools.partial(_dil_rows_kernel, length=s),
        grid=(b, s // tl),
        in_specs=[pl.BlockSpec(memory_space=pltpu.SMEM), cur, prev, cur, nxt, prev, cur, nxt],
        out_specs=(cur, pl.BlockSpec((1, tl, LANES), lambda bi, i: (bi, i, 0))),
        out_shape=(jax.ShapeDtypeStruct((b, s, w), BF16), jax.ShapeDtypeStruct((b, s, LANES), F32)),
        scratch_shapes=[
            pltpu.VMEM((tl + 2 * DIL_BLOCK, w), BF16),
            pltpu.VMEM((tl + 2 * DIL_BLOCK, w), BF16),
            pltpu.VMEM((N_HEADS_B, DIL_SUB, DIL_SUB + 2 * DIL_BLOCK), F32),
        ],
        compiler_params=pltpu.CompilerParams(
            dimension_semantics=("arbitrary", "arbitrary"), vmem_limit_bytes=VMEM_LIMIT_BYTES),
        name="dil_1",
    )(bounded, q, k, k, k, v, v, v)


def _dil_grouped_kernel(bounded_ref, q_ref, k_ref, v_ref, o_ref, lse_ref,
                        qflat_ref, kext_ref, vext_ref, oflat_ref, lseflat_ref, bias_ref, *, dilation):
    n_tiles, rows = q_ref.shape[1], q_ref.shape[2]
    length = n_tiles * rows
    bi, r = pl.program_id(0), pl.program_id(1)

    @pl.when((bi == 0) & (r == 0))
    def _():
        _dil_bias_init(bias_ref, dilation)

    border = jnp.zeros((DIL_BLOCK, kext_ref.shape[1]), BF16)
    for ext_ref in (kext_ref, vext_ref):
        ext_ref[0:DIL_BLOCK] = border
        ext_ref[DIL_BLOCK + length:] = border
    for t in range(n_tiles):
        qflat_ref[rows * t:rows * (t + 1)] = q_ref[0, t]
        kext_ref[DIL_BLOCK + rows * t:DIL_BLOCK + rows * (t + 1)] = k_ref[0, t]
        vext_ref[DIL_BLOCK + rows * t:DIL_BLOCK + rows * (t + 1)] = v_ref[0, t]
    _dil_sub_tiles(bounded_ref[0] != 0, qflat_ref, kext_ref, vext_ref, oflat_ref, lseflat_ref, bias_ref,
                   n_sub=length // DIL_SUB, first_key=-DIL_BLOCK, length=length)
    for t in range(n_tiles):
        o_ref[0, t] = oflat_ref[rows * t:rows * (t + 1)]
        lse_ref[0, t] = lseflat_ref[rows * t:rows * (t + 1)]


def _dilated_grouped(bounded, q, k, v, dilation):
    b, s, w = q.shape
    n_tiles, rows = s // GROUP_TILE, GROUP_TILE // dilation
    length = n_tiles * rows
    view = lambda a: a.reshape(b, n_tiles, dilation, rows, a.shape[-1])
    spec = lambda width: pl.BlockSpec((1, n_tiles, None, rows, width), lambda bi, r: (bi, 0, r, 0, 0))
    o, lse = pl.pallas_call(
        functools.partial(_dil_grouped_kernel, dilation=dilation),
        grid=(b, dilation),
        in_specs=[pl.BlockSpec(memory_space=pltpu.SMEM), spec(w), spec(w), spec(w)],
        out_specs=(spec(w), spec(LANES)),
        out_shape=(jax.ShapeDtypeStruct((b, n_tiles, dilation, rows, w), BF16),
                   jax.ShapeDtypeStruct((b, n_tiles, dilation, rows, LANES), F32)),
        scratch_shapes=[
            pltpu.VMEM((length, w), BF16),
            pltpu.VMEM((length + 2 * DIL_BLOCK, w), BF16),
            pltpu.VMEM((length + 2 * DIL_BLOCK, w), BF16),
            pltpu.VMEM((length, w), BF16),
            pltpu.VMEM((length, LANES), F32),
            pltpu.VMEM((N_HEADS_B, DIL_SUB, DIL_SUB + 2 * DIL_BLOCK), F32),
        ],
        compiler_params=pltpu.CompilerParams(
            dimension_semantics=("arbitrary", "arbitrary"), vmem_limit_bytes=VMEM_LIMIT_BYTES),
        name=f"dil_{dilation}",
    )(bounded, view(q), view(k), view(v))
    return o.reshape(b, s, w), lse.reshape(b, s, LANES)


def _ungroup_rows(val, levels, scratch_refs):
    slabs = val.shape[1] // LANES
    for lvl in reversed(range(levels)):
        block = GROUP_TILE // GROUP_STEP ** lvl
        sub = block // GROUP_STEP
        ref = scratch_refs[lvl]
        for j in range(slabs):
            for lo in range(0, GROUP_TILE, sub):
                blk0, r = (lo // block) * block, (lo % block) // sub
                ref[j, pl.ds(blk0 + r, sub, stride=GROUP_STEP), :] = val[lo:lo + sub, LANES * j:LANES * (j + 1)]
        val = jnp.concatenate([ref[j] for j in range(slabs)], axis=1)
    return val


def _merge_kernel(x_ref, oat_ref, o1_ref, o2_ref, o3_ref, l1_ref, l2_ref, l3_ref,
                  ga_ref, gb_ref, wo_ref, out_ref, *scratch):
    def group_norm(y, g_ref):
        ms = jnp.mean(y * y, axis=-1, keepdims=True)
        return (y * lax.rsqrt(ms + NORM_EPS) * g_ref[...]).astype(BF16)

    oa = oat_ref[0].astype(F32).T

    o2_scr, o3_scr, l2_scr, l3_scr = scratch[0:1], scratch[1:3], scratch[3:4], scratch[4:6]

    def limbs(val):
        hi = val.astype(BF16)
        return hi, (val - hi.astype(F32)).astype(BF16)

    l1, l2, l3 = l1_ref[0], _ungroup_rows(l2_ref[0], 1, l2_scr), _ungroup_rows(l3_ref[0], 2, l3_scr)
    mx = jnp.maximum(jnp.maximum(l1, l2), l3)
    e1, e2, e3 = jnp.exp(l1 - mx), jnp.exp(l2 - mx), jnp.exp(l3 - mx)
    inv = 1.0 / (e1 + e2 + e3)
    ri = lax.broadcasted_iota(jnp.int32, (2 * LANES, WIDTH_B), 0) % LANES
    ci = lax.broadcasted_iota(jnp.int32, (2 * LANES, WIDTH_B), 1)
    expand = jnp.where(ri == _lse_lane_of_head(ci // HEAD_DIM), 1.0, 0.0).astype(BF16)

    def widen(wgt):
        return jnp.dot(jnp.concatenate(limbs(wgt), axis=1), expand, preferred_element_type=F32)

    ob = (widen(e1 * inv) * o1_ref[0].astype(F32)
          + widen(e2 * inv) * _ungroup_rows(o2_ref[0].astype(F32), 1, o2_scr)
          + widen(e3 * inv) * _ungroup_rows(o3_ref[0].astype(F32), 2, o3_scr))
    mixed_a = group_norm(oa, ga_ref)
    mixed_b = group_norm(ob, gb_ref)
    out_ref[0] = (x_ref[0]
                  + jnp.dot(mixed_a, wo_ref[:WIDTH_A], preferred_element_type=F32)
                  + jnp.dot(mixed_b, wo_ref[WIDTH_A:], preferred_element_type=F32))


def _merge(x, oat, obs, lses, ga, gb, wo, grouped_dilations):
    b, s, d = x.shape
    tm = GROUP_TILE
    assert tuple(grouped_dilations) == (GROUP_STEP, GROUP_STEP ** 2)
    slab = lambda n: pltpu.VMEM((n, tm, LANES), F32)
    wide, narrow = WIDTH_B // LANES, 1
    full = lambda shape: pl.BlockSpec(shape, lambda bi, i: (0,) * len(shape))
    row = lambda w: pl.BlockSpec((1, tm, w), lambda bi, i: (bi, i, 0))
    return pl.pallas_call(
        _merge_kernel,
        grid=(b, s // tm),
        in_specs=[row(d), pl.BlockSpec((1, WIDTH_A, tm), lambda bi, i: (bi, 0, i)),
                  row(WIDTH_B), row(WIDTH_B), row(WIDTH_B), row(LANES), row(LANES), row(LANES),
                  full(ga.shape), full(gb.shape), full(wo.shape)],
        out_specs=row(d),
        out_shape=jax.ShapeDtypeStruct((b, s, d), F32),
        scratch_shapes=[slab(wide), slab(wide), slab(wide), slab(narrow), slab(narrow), slab(narrow)],
        compiler_params=pltpu.CompilerParams(
            dimension_semantics=("parallel", "parallel"), vmem_limit_bytes=VMEM_LIMIT_BYTES),
        name="merge",
    )(x, oat, *obs, *lses, ga, gb, wo)


def _gelu_tanh(x):
    c = math.sqrt(2.0 / math.pi)
    return 0.5 * x * (1.0 + jnp.tanh(c * (x + 0.044715 * (x * x * x))))


def _ffn_kernel(xp_ref, xc_ref, xn_ref, g2_ref, wup_ref, cw_ref, cb_ref, wdn_ref, out_ref,
                hext_ref, act_ref, *, d_ff):
    tm = xc_ref.shape[1]
    halo = xp_ref.shape[1]
    i = pl.program_id(1)
    rows = tm + 2 * halo

    def normed(x):
        ms = jnp.mean(x * x, axis=-1, keepdims=True)
        return x * lax.rsqrt(ms + NORM_EPS) * g2_ref[...]

    keep_prev = (i > 0).astype(F32)
    keep_next = (i < pl.num_programs(1) - 1).astype(F32)
    hext_ref[0:halo] = (normed(xp_ref[0]) * keep_prev).astype(BF16)
    hext_ref[halo:halo + tm] = normed(xc_ref[0]).astype(BF16)
    hext_ref[halo + tm:] = (normed(xn_ref[0]) * keep_next).astype(BF16)
    hext = hext_ref[...]

    def conv(u, col):
        w = cw_ref[:, col:col + FFN_FC]
        up = pltpu.roll(u, 1, 0)
        dn = pltpu.roll(u, rows - 1, 0)
        y = up * w[0:1] + u * w[1:2] + dn * w[2:3] + cb_ref[:, col:col + FFN_FC]
        return y[halo:halo + tm]

    for c in range(0, d_ff, FFN_FC):
        ug = jnp.dot(hext, wup_ref[:, c:c + FFN_FC], preferred_element_type=F32)
        uv = jnp.dot(hext, wup_ref[:, d_ff + c:d_ff + c + FFN_FC], preferred_element_type=F32)
        act_ref[:, c:c + FFN_FC] = (_gelu_tanh(conv(ug, c)) * conv(uv, d_ff + c)).astype(BF16)

    out_ref[0] = xc_ref[0] + jnp.dot(act_ref[...], wdn_ref[...], preferred_element_type=F32)


def _ffn(x, g2, w_up, conv_w, conv_b, w_down):
    b, s, d = x.shape
    d_ff = w_down.shape[0]
    tm, halo = FFN_TM, FFN_HALO
    per_tile = tm // halo
    n_halo_blocks = s // halo
    full = lambda shape: pl.BlockSpec(shape, lambda bi, i: (0,) * len(shape))
    resident = lambda shape: pl.BlockSpec(shape, lambda bi, i: (0,) * len(shape), pipeline_mode=pl.Buffered(1))
    return pl.pallas_call(
        functools.partial(_ffn_kernel, d_ff=d_ff),
        grid=(b, s // tm),
        in_specs=[
            pl.BlockSpec((1, halo, d), lambda bi, i: (bi, jnp.maximum(i * per_tile - 1, 0), 0)),
            pl.BlockSpec((1, tm, d), lambda bi, i: (bi, i, 0)),
            pl.BlockSpec((1, halo, d), lambda bi, i: (bi, jnp.minimum((i + 1) * per_tile, n_halo_blocks - 1), 0)),
            full(g2.shape), resident(w_up.shape), full(conv_w.shape), full(conv_b.shape), resident(w_down.shape),
        ],
        out_specs=pl.BlockSpec((1, tm, d), lambda bi, i: (bi, i, 0)),
        out_shape=jax.ShapeDtypeStruct((b, s, d), F32),
        scratch_shapes=[pltpu.VMEM((tm + 2 * halo, d), BF16), pltpu.VMEM((tm, d_ff), BF16)],
        compiler_params=pltpu.CompilerParams(
            dimension_semantics=("parallel", "parallel"), vmem_limit_bytes=VMEM_LIMIT_BYTES),
        name="ffn",
    )(x, x, x, g2, w_up, conv_w, conv_b, w_down)


def _rope_tables_t(seq_len):
    rows = seq_len // GRID_W
    row = np.repeat(np.arange(rows, dtype=np.float32), GRID_W)
    col = np.tile(np.arange(GRID_W, dtype=np.float32), rows)
    inv = (np.float32(ROPE_THETA) ** (-np.arange(0, ROPE_AXIS_DIM, 2, dtype=np.float32) / np.float32(ROPE_AXIS_DIM))
           ).astype(np.float32)
    ang_r = (row[:, None] * inv[None, :]).T
    ang_c = (col[:, None] * inv[None, :]).T
    cos_t = np.concatenate([np.cos(ang_r)] * 2 + [np.cos(ang_c)] * 2, axis=0).astype(np.float32)
    sin_t = np.concatenate([-np.sin(ang_r), np.sin(ang_r), -np.sin(ang_c), np.sin(ang_c)], axis=0).astype(np.float32)
    return jnp.asarray(cos_t), jnp.asarray(sin_t)


def _score_bound(gq, gk):
    return math.sqrt(HEAD_DIM) * jnp.max(jnp.abs(gq)) * jnp.max(jnp.abs(gk))


def kernel(x, norm1_g, w_in, qa_norm_g, ka_norm_g, qb_norm_g, kb_norm_g, outa_norm_g, outb_norm_g,
           w_out, norm2_g, w_up, conv_w, conv_b, w_down):
    b, s, d = x.shape
    scale = HEAD_DIM ** -0.5 * math.log2(math.e)
    a_cols = WIDTH_A + 2 * KV_WIDTH_A
    wa = w_in[:, :a_cols].astype(BF16)
    wb = w_in[:, a_cols:].astype(BF16)
    bound_a = _score_bound(qa_norm_g, ka_norm_g)
    bounded_a = (bound_a <= SAFE_SCORE_BOUND).astype(jnp.int32).reshape(1)
    rescale_a = jnp.maximum(bound_a / SAFE_SCORE_BOUND, 1.0)
    positive = lambda g: jnp.where(g > 0, g, 1.0)
    gq_max, gk_max = positive(jnp.max(jnp.abs(qa_norm_g))), positive(jnp.max(jnp.abs(ka_norm_g)))
    ga = jnp.concatenate([jnp.tile(qa_norm_g, N_HEADS_A) * jnp.sqrt(scale * gk_max / (gq_max * rescale_a)),
                          jnp.tile(ka_norm_g, N_KV_A) * jnp.sqrt(scale * gq_max / (gk_max * rescale_a))])
    ga = jnp.broadcast_to(ga[:, None], (ga.shape[0], LANES))
    gqb = (jnp.tile(qb_norm_g, N_HEADS_B) * scale)[None, :]
    gkb = jnp.tile(kb_norm_g, N_HEADS_B)[None, :]
    cos_t, sin_t = _rope_tables_t(s)

    dilations = [dil for _, dil in DILATED_PATTERNS]
    assert dilations[0] == 1
    grouped_dilations = dilations[1:]
    qt, k, vt, *qkv_b = _proj(x, norm1_g[None, :], wa, wb, ga, cos_t, sin_t, gqb, gkb, grouped_dilations)

    oat = _attn_a(bounded_a, rescale_a.reshape(1), qt, k, vt)
    bounded_b = (_score_bound(qb_norm_g, kb_norm_g) <= SAFE_SCORE_BOUND).astype(jnp.int32).reshape(1)
    results = [_dilated_rows(bounded_b, *qkv_b[:3])]
    for n, dil in enumerate(grouped_dilations):
        results.append(_dilated_grouped(bounded_b, *qkv_b[3 * (n + 1):3 * (n + 2)], dil))
    obs, lses = zip(*results)
    x2 = _merge(x, oat, obs, lses, outa_norm_g[None, :], outb_norm_g[None, :], w_out.astype(BF16),
                grouped_dilations)
    return _ffn(x2, norm2_g[None, :], w_up.astype(BF16), conv_w, conv_b[None, :], w_down.astype(BF16))
```

```python
import functools
import math

import jax
import jax.numpy as jnp
import numpy as np
from jax import lax
from jax.experimental import pallas as pl
from jax.experimental.pallas import tpu as pltpu

F32 = jnp.float32
BF16 = jnp.bfloat16
FP8 = jnp.float8_e4m3fn

HEAD_DIM = 64
N_HEADS_A = 8
N_KV_A = 2
GQA_GROUP = N_HEADS_A // N_KV_A
N_HEADS_B = 8
WIDTH_A = N_HEADS_A * HEAD_DIM
WIDTH_B = N_HEADS_B * HEAD_DIM
KV_WIDTH_A = N_KV_A * HEAD_DIM
GRID_W = 64
ROPE_THETA = 10000.0
ROPE_AXIS_DIM = HEAD_DIM // 2
DIL_BLOCK = 64
DILATED_PATTERNS = ((128, 1), (512, 4), (2048, 16))
NORM_EPS = 1e-6
NEG_INF = -1e30

LANES = 128
MXU_DIM = 256
VMEM_LIMIT_BYTES = 56 * 1024 * 1024

PROJ_TM = 1024
GROUP_TILE = 512
GROUP_STEP = 4
ATT_TQ = 2048
ATT_TK = 2048
SAFE_SCORE_BOUND = 40.0
VT_ROWS = HEAD_DIM + 16
DIL_TL = 1024
DIL_SUB = 128
DIL_UNROLL = 4
MERGE_TM = 1024
FFN_TM = 1024
FFN_FC = 256
FFN_HALO = 16
LSE_LANES = 16


def _nt_dot(a, b):
    return lax.dot_general(a, b, (((1,), (1,)), ((), ())), preferred_element_type=F32)


def _proj_kernel(x_ref, g1_ref, wa_ref, wb_ref, ga_ref, cos_ref, sin_ref, gqb_ref, gkb_ref,
                 qt_ref, k_ref, vt_ref, qb_ref, kb_ref, vb_ref, *rest):
    *grouped_refs, nat_all_ref, lvl_all_ref = rest
    for part in range(x_ref.shape[1] // GROUP_TILE):
        _proj_rows(part * GROUP_TILE, x_ref, g1_ref, wa_ref, wb_ref, ga_ref, cos_ref, sin_ref, gqb_ref, gkb_ref,
                   qt_ref, k_ref, vt_ref, qb_ref, kb_ref, vb_ref, grouped_refs,
                   nat_all_ref.at[part], lvl_all_ref.at[part])


def _proj_rows(r0, x_ref, g1_ref, wa_ref, wb_ref, ga_ref, cos_ref, sin_ref, gqb_ref, gkb_ref,
               qt_ref, k_ref, vt_ref, qb_ref, kb_ref, vb_ref, grouped_refs, nat_ref, lvl_ref):
    tm = GROUP_TILE
    rows = slice(r0, r0 + tm)
    x = x_ref[0, rows]
    ms = jnp.mean(x * x, axis=-1, keepdims=True)
    hn = (x * lax.rsqrt(ms + NORM_EPS) * g1_ref[...]).astype(BF16)
    ya = jnp.dot(hn, wa_ref[...], preferred_element_type=F32)
    yb = jnp.dot(hn, wb_ref[...], preferred_element_type=F32)

    ya_t = ya.T
    ones_rows = jnp.where(lax.broadcasted_iota(jnp.int32, (VT_ROWS - HEAD_DIM, tm), 0) == 0, 1.0, 0.0).astype(BF16)
    for g in range(N_KV_A):
        v_rows = ya_t[WIDTH_A + KV_WIDTH_A + HEAD_DIM * g:WIDTH_A + KV_WIDTH_A + HEAD_DIM * (g + 1)]
        vt_ref[0, g, :, rows] = jnp.concatenate([v_rows.astype(BF16), ones_rows], axis=0)
    cos_t = cos_ref[:, rows]
    sin_t = sin_ref[:, rows]
    reps = tm // LANES
    zeros = jnp.zeros((HEAD_DIM, tm), F32)
    q16 = ROPE_AXIS_DIM // 2

    def limbs8(val):
        hi = val.astype(FP8).astype(F32)
        return hi, (val - hi).astype(FP8).astype(F32)

    for h in range(N_HEADS_A + N_KV_A):
        u = ya_t[HEAD_DIM * h:HEAD_DIM * (h + 1)]
        r = lax.rsqrt(jnp.mean(u * u, axis=0, keepdims=True) + NORM_EPS)
        g = jnp.tile(ga_ref[HEAD_DIM * h:HEAD_DIM * (h + 1), :], (1, reps))
        un = u * r * g
        partner = jnp.concatenate(
            [un[q16:2 * q16], un[0:q16], un[3 * q16:4 * q16], un[2 * q16:3 * q16]], axis=0)
        hi, lo = limbs8(un * cos_t + partner * sin_t)
        if h < N_HEADS_A:
            qt_ref[0, h, :, rows] = jnp.concatenate([hi, lo, hi, zeros], axis=0).astype(FP8)
        else:
            k_ref[0, h - N_HEADS_A, rows] = jnp.concatenate([hi, hi, lo, zeros], axis=0).T.astype(FP8)

    ri = lax.broadcasted_iota(jnp.int32, (MXU_DIM, MXU_DIM), 0) // HEAD_DIM
    ci = lax.broadcasted_iota(jnp.int32, (MXU_DIM, MXU_DIM), 1) // HEAD_DIM
    ones_bd = jnp.where(ri == ci, 1.0, 0.0).astype(BF16)

    def head_norm(y, g_ref):
        sq = (y * y).astype(BF16)
        ss = jnp.concatenate(
            [jnp.dot(sq[:, c:c + MXU_DIM], ones_bd, preferred_element_type=F32)
             for c in range(0, y.shape[1], MXU_DIM)], axis=1)
        return y * lax.rsqrt(ss * (1.0 / HEAD_DIM) + NORM_EPS) * g_ref[...]

    parts = (head_norm(yb[:, :WIDTH_B], gqb_ref), head_norm(yb[:, WIDTH_B:2 * WIDTH_B], gkb_ref),
             yb[:, 2 * WIDTH_B:])
    slabs = WIDTH_B // LANES
    for c, (part, ref) in enumerate(zip(parts, (qb_ref, kb_ref, vb_ref))):
        ref[0, rows] = part.astype(BF16)
        for jj in range(slabs):
            nat_ref[slabs * c + jj] = part[:, LANES * jj:LANES * (jj + 1)]

    src_ref, block = nat_ref, tm
    n_levels = len(grouped_refs) // 3
    for n in range(n_levels):
        sub = block // GROUP_STEP
        for j in range(3 * slabs):
            c, jj = divmod(j, slabs)
            dst_ref = grouped_refs[3 * n + c]
            for lo in range(0, tm, sub):
                blk0, r = (lo // block) * block, (lo % block) // sub
                picked = src_ref[j, pl.ds(blk0 + r, sub, stride=GROUP_STEP), :]
                if n + 1 < n_levels:
                    lvl_ref[j, lo:lo + sub, :] = picked
                dst_ref[0, r0 + lo:r0 + lo + sub, LANES * jj:LANES * (jj + 1)] = picked.astype(BF16)
        src_ref, block = lvl_ref, sub


def _group_index(tau, tm, dilation):
    base, size, local = 0, tm, tau
    while dilation > 1:
        size //= GROUP_STEP
        base = base + (local % GROUP_STEP) * size
        local = local // GROUP_STEP
        dilation //= GROUP_STEP
    return base + local


def _proj(x, g1, wa, wb, ga, cos_t, sin_t, gqb, gkb, grouped_dilations):
    b, s, d = x.shape
    tm = PROJ_TM
    assert tuple(grouped_dilations) == tuple(GROUP_STEP ** (n + 1) for n in range(len(grouped_dilations)))
    assert len(grouped_dilations) <= 2
    full = lambda shape: pl.BlockSpec(shape, lambda bi, i: (0,) * len(shape))
    rows_b = jax.ShapeDtypeStruct((b, s, WIDTH_B), BF16)
    out_shape = (
        jax.ShapeDtypeStruct((b, N_HEADS_A, MXU_DIM, s), FP8),
        jax.ShapeDtypeStruct((b, N_KV_A, s, MXU_DIM), FP8),
        jax.ShapeDtypeStruct((b, N_KV_A, VT_ROWS, s), BF16),
    ) + (rows_b,) * (3 + 3 * len(grouped_dilations))
    row = lambda w: pl.BlockSpec((1, tm, w), lambda bi, i: (bi, i, 0))
    return pl.pallas_call(
        _proj_kernel,
        grid=(b, s // tm),
        in_specs=[
            row(d), full(g1.shape), full(wa.shape), full(wb.shape), full(ga.shape),
            pl.BlockSpec((HEAD_DIM, tm), lambda bi, i: (0, i)),
            pl.BlockSpec((HEAD_DIM, tm), lambda bi, i: (0, i)),
            full(gqb.shape), full(gkb.shape),
        ],
        out_specs=(
            pl.BlockSpec((1, N_HEADS_A, MXU_DIM, tm), lambda bi, i: (bi, 0, 0, i)),
            pl.BlockSpec((1, N_KV_A, tm, MXU_DIM), lambda bi, i: (bi, 0, i, 0)),
            pl.BlockSpec((1, N_KV_A, VT_ROWS, tm), lambda bi, i: (bi, 0, 0, i)),
        ) + (row(WIDTH_B),) * (3 + 3 * len(grouped_dilations)),
        out_shape=out_shape,
        scratch_shapes=[pltpu.VMEM((tm // GROUP_TILE, 3 * WIDTH_B // LANES, GROUP_TILE, LANES), F32)] * 2,
        compiler_params=pltpu.CompilerParams(
            dimension_semantics=("parallel", "parallel"), vmem_limit_bytes=VMEM_LIMIT_BYTES),
        name="proj",
    )(x, g1, wa, wb, ga, cos_t, sin_t, gqb, gkb)


def _attn_a_kernel(bounded_ref, rescale_ref, qt_ref, k_ref, vt_ref, ot_ref, m_ref, acc_ref):
    j = pl.program_id(3)
    bounded = bounded_ref[0] != 0

    @pl.when(j == 0)
    def _():
        m_ref[...] = jnp.full(m_ref.shape, NEG_INF, F32)
        acc_ref[...] = jnp.zeros(acc_ref.shape, F32)

    k = k_ref[0, 0]
    vt = vt_ref[0, 0]

    def scores(h):
        return jnp.dot(k, qt_ref[0, h], preferred_element_type=F32) * rescale_ref[0]

    @pl.when(bounded)
    def _():
        tk, tq = k_ref.shape[2], qt_ref.shape[3]
        for h in range(GQA_GROUP):
            for c in range(0, tq, MXU_DIM):
                q_c = qt_ref[0, h, :, c:c + MXU_DIM]
                part = None
                for r in range(0, tk, MXU_DIM):
                    s_t = jnp.dot(k_ref[0, 0, r:r + MXU_DIM, :], q_c, preferred_element_type=F32)
                    pv = jnp.dot(vt_ref[0, 0, :, r:r + MXU_DIM], jnp.exp2(s_t).astype(BF16),
                                 preferred_element_type=F32)
                    part = pv if part is None else part + pv
                acc_ref[h, :, c:c + MXU_DIM] += part

    @pl.when(jnp.logical_not(bounded))
    def _():
        for h in range(GQA_GROUP):
            s_t = scores(h)
            m_prev = m_ref[h]
            m_new = jnp.maximum(m_prev, jnp.max(s_t, axis=0, keepdims=True))
            p_t = jnp.exp2(s_t - m_new).astype(BF16)
            acc_ref[h] = (jnp.exp2(m_prev - m_new) * acc_ref[h]
                          + jnp.dot(vt, p_t, preferred_element_type=F32))
            m_ref[h] = m_new

    @pl.when(j == pl.num_programs(3) - 1)
    def _():
        for h in range(GQA_GROUP):
            a = acc_ref[h]
            ot_ref[0, HEAD_DIM * h:HEAD_DIM * (h + 1), :] = (
                a[:HEAD_DIM] * (1.0 / a[HEAD_DIM:HEAD_DIM + 1])).astype(BF16)


def _attn_a(bounded, rescale, qt, k, vt):
    b, _, _, s = qt.shape
    tq, tk = ATT_TQ, ATT_TK
    gw = GQA_GROUP * HEAD_DIM
    return pl.pallas_call(
        _attn_a_kernel,
        grid=(b, N_KV_A, s // tq, s // tk),
        in_specs=[
            pl.BlockSpec(memory_space=pltpu.SMEM),
            pl.BlockSpec(memory_space=pltpu.SMEM),
            pl.BlockSpec((1, GQA_GROUP, MXU_DIM, tq), lambda bi, g, i, j: (bi, g, 0, i)),
            pl.BlockSpec((1, 1, tk, MXU_DIM), lambda bi, g, i, j: (bi, g, j, 0)),
            pl.BlockSpec((1, 1, VT_ROWS, tk), lambda bi, g, i, j: (bi, g, 0, j)),
        ],
        out_specs=pl.BlockSpec((1, gw, tq), lambda bi, g, i, j: (bi, g, i)),
        out_shape=jax.ShapeDtypeStruct((b, WIDTH_A, s), BF16),
        scratch_shapes=[
            pltpu.VMEM((GQA_GROUP, 1, tq), F32),
            pltpu.VMEM((GQA_GROUP, VT_ROWS, tq), F32),
        ],
        compiler_params=pltpu.CompilerParams(
            dimension_semantics=("parallel", "parallel", "parallel", "arbitrary"),
            vmem_limit_bytes=VMEM_LIMIT_BYTES),
        name="attn_a",
    )(bounded, rescale, qt, k, vt)


def _dil_bias_init(bias_ref, dilation):
    tk = DIL_SUB + 2 * DIL_BLOCK
    qi = lax.broadcasted_iota(jnp.int32, (DIL_SUB, tk), 0)
    kj = lax.broadcasted_iota(jnp.int32, (DIL_SUB, tk), 1)
    aoff = jnp.abs(kj - DIL_BLOCK - qi)
    dist = (aoff * dilation).astype(F32)
    for h in range(N_HEADS_B):
        slope = 2.0 ** (-8.0 * (h + 1) / N_HEADS_B) * math.log2(math.e)
        bias_ref[h] = jnp.where(aoff <= DIL_BLOCK, -slope * dist, NEG_INF)


def _lse_lane_pair(lane):
    return (lane % HEAD_DIM) // LSE_LANES


def _lse_lane_of_head(h):
    return HEAD_DIM * (h % 2) + LSE_LANES * (h // 2)


def _dil_sub_tiles(bounded, q_ref, kext_ref, vext_ref, o_ref, lse_ref, bias_ref, *, n_sub, first_key, length):
    tk = DIL_SUB + 2 * DIL_BLOCK
    first_head = lax.broadcasted_iota(jnp.int32, (DIL_SUB, LANES), 1) < HEAD_DIM
    first_head_k = lax.broadcasted_iota(jnp.int32, (tk, LANES), 1) < HEAD_DIM
    lse_owner = _lse_lane_pair(lax.broadcasted_iota(jnp.int32, (DIL_SUB, LANES), 1))
    key_lane = lax.broadcasted_iota(jnp.int32, (1, tk), 1)
    stack_row = lax.broadcasted_iota(jnp.int32, (2 * tk, LANES), 0) < tk
    stack_lane = lax.broadcasted_iota(jnp.int32, (2 * tk, LANES), 1) < HEAD_DIM
    ones_stack = jnp.where(stack_row == stack_lane, 1.0, 0.0).astype(BF16)

    def sub_tile(st, carry, *, use_max):
        r0 = pl.multiple_of(st * DIL_SUB, DIL_SUB)
        kidx = first_key + r0 + key_lane
        in_range = (kidx >= 0) & (kidx < length)
        lse_tile = jnp.zeros((DIL_SUB, LANES), F32)
        for hp in range(N_HEADS_B // 2):
            cols = slice(LANES * hp, LANES * (hp + 1))
            qp = q_ref[pl.ds(r0, DIL_SUB), cols]
            kp = kext_ref[pl.ds(r0, tk), cols]
            vp = vext_ref[pl.ds(r0, tk), cols]
            zeros_v = jnp.zeros_like(vp)
            v_stack = jnp.concatenate([jnp.where(first_head_k, vp, zeros_v),
                                       jnp.where(first_head_k, zeros_v, vp)], axis=0)
            ps, ms = [], []
            for e in range(2):
                own = first_head if e == 0 else jnp.logical_not(first_head)
                qm = jnp.where(own, qp, jnp.zeros_like(qp))
                s = _nt_dot(qm, kp) + bias_ref[2 * hp + e]
                s = jnp.where(in_range, s, NEG_INF)
                if use_max:
                    ms.append(jnp.max(s, axis=-1, keepdims=True))
                    s = s - ms[-1]
                ps.append(jnp.exp2(s).astype(BF16))
            p_pair = jnp.concatenate(ps, axis=1)
            both = jnp.dot(p_pair, jnp.concatenate([v_stack, ones_stack], axis=1), preferred_element_type=F32)
            pv, l_pair = both[:, :LANES], both[:, LANES:]
            o_ref[pl.ds(r0, DIL_SUB), cols] = (pv * (1.0 / l_pair)).astype(BF16)
            lse = jnp.log(l_pair)
            if use_max:
                lse = lse + jnp.where(first_head, ms[0], ms[1]) * math.log(2.0)
            lse_tile = jnp.where(lse_owner == hp, lse, lse_tile)
        lse_ref[pl.ds(r0, DIL_SUB), :] = lse_tile
        return carry

    @pl.when(bounded)
    def _():
        lax.fori_loop(0, n_sub, functools.partial(sub_tile, use_max=False), 0, unroll=DIL_UNROLL)

    @pl.when(jnp.logical_not(bounded))
    def _():
        lax.fori_loop(0, n_sub, functools.partial(sub_tile, use_max=True), 0)


def _dil_rows_kernel(bounded_ref, q_ref, kp_ref, kc_ref, kn_ref, vp_ref, vc_ref, vn_ref, o_ref, lse_ref,
                     kext_ref, vext_ref, bias_ref, *, length):
    tl = q_ref.shape[1]
    bi, i = pl.program_id(0), pl.program_id(1)

    @pl.when((bi == 0) & (i == 0))
    def _():
        _dil_bias_init(bias_ref, 1)

    kext_ref[0:DIL_BLOCK] = kp_ref[0]
    kext_ref[DIL_BLOCK:DIL_BLOCK + tl] = kc_ref[0]
    kext_ref[DIL_BLOCK + tl:] = kn_ref[0]
    vext_ref[0:DIL_BLOCK] = vp_ref[0]
    vext_ref[DIL_BLOCK:DIL_BLOCK + tl] = vc_ref[0]
    vext_ref[DIL_BLOCK + tl:] = vn_ref[0]
    _dil_sub_tiles(bounded_ref[0] != 0, q_ref.at[0], kext_ref, vext_ref, o_ref.at[0], lse_ref.at[0], bias_ref,
                   n_sub=tl // DIL_SUB, first_key=i * tl - DIL_BLOCK, length=length)


def _dilated_rows(bounded, q, k, v):
    b, s, w = q.shape
    tl = DIL_TL
    blocks_per_tile = tl // DIL_BLOCK
    n_blocks = s // DIL_BLOCK
    cur = pl.BlockSpec((1, tl, w), lambda bi, i: (bi, i, 0))
    prev = pl.BlockSpec((1, DIL_BLOCK, w), lambda bi, i: (bi, jnp.maximum(i * blocks_per_tile - 1, 0), 0))
    nxt = pl.BlockSpec((1, DIL_BLOCK, w),
                       lambda bi, i: (bi, jnp.minimum((i + 1) * blocks_per_tile, n_blocks - 1), 0))
    return pl.pallas_call(
        functools.partial(_dil_rows_kernel, length=s),
        grid=(b, s // tl),
        in_specs=[pl.BlockSpec(memory_space=pltpu.SMEM), cur, prev, cur, nxt, prev, cur, nxt],
        out_specs=(cur, pl.BlockSpec((1, tl, LANES), lambda bi, i: (bi, i, 0))),
        out_shape=(jax.ShapeDtypeStruct((b, s, w), BF16), jax.ShapeDtypeStruct((b, s, LANES), F32)),
        scratch_shapes=[
            pltpu.VMEM((tl + 2 * DIL_BLOCK, w), BF16),
            pltpu.VMEM((tl + 2 * DIL_BLOCK, w), BF16),
            pltpu.VMEM((N_HEADS_B, DIL_SUB, DIL_SUB + 2 * DIL_BLOCK), F32),
        ],
        compiler_params=pltpu.CompilerParams(
            dimension_semantics=("arbitrary", "arbitrary"), vmem_limit_bytes=VMEM_LIMIT_BYTES),
        name="dil_1",
    )(bounded, q, k, k, k, v, v, v)


def _dil_grouped_kernel(bounded_ref, q_ref, k_ref, v_ref, o_ref, lse_ref,
                        qflat_ref, kext_ref, vext_ref, oflat_ref, lseflat_ref, bias_ref, *, dilation):
    n_tiles, rows = q_ref.shape[1], q_ref.shape[2]
    length = n_tiles * rows
    bi, r = pl.program_id(0), pl.program_id(1)

    @pl.when((bi == 0) & (r == 0))
    def _():
        _dil_bias_init(bias_ref, dilation)

    border = jnp.zeros((DIL_BLOCK, kext_ref.shape[1]), BF16)
    for ext_ref in (kext_ref, vext_ref):
        ext_ref[0:DIL_BLOCK] = border
        ext_ref[DIL_BLOCK + length:] = border
    for t in range(n_tiles):
        qflat_ref[rows * t:rows * (t + 1)] = q_ref[0, t]
        kext_ref[DIL_BLOCK + rows * t:DIL_BLOCK + rows * (t + 1)] = k_ref[0, t]
        vext_ref[DIL_BLOCK + rows * t:DIL_BLOCK + rows * (t + 1)] = v_ref[0, t]
    _dil_sub_tiles(bounded_ref[0] != 0, qflat_ref, kext_ref, vext_ref, oflat_ref, lseflat_ref, bias_ref,
                   n_sub=length // DIL_SUB, first_key=-DIL_BLOCK, length=length)
    for t in range(n_tiles):
        o_ref[0, t] = oflat_ref[rows * t:rows * (t + 1)]
        lse_ref[0, t] = lseflat_ref[rows * t:rows * (t + 1)]


def _dilated_grouped(bounded, q, k, v, dilation):
    b, s, w = q.shape
    n_tiles, rows = s // GROUP_TILE, GROUP_TILE // dilation
    length = n_tiles * rows
    view = lambda a: a.reshape(b, n_tiles, dilation, rows, a.shape[-1])
    spec = lambda width: pl.BlockSpec((1, n_tiles, None, rows, width), lambda bi, r: (bi, 0, r, 0, 0))
    o, lse = pl.pallas_call(
        functools.partial(_dil_grouped_kernel, dilation=dilation),
        grid=(b, dilation),
        in_specs=[pl.BlockSpec(memory_space=pltpu.SMEM), spec(w), spec(w), spec(w)],
        out_specs=(spec(w), spec(LANES)),
        out_shape=(jax.ShapeDtypeStruct((b, n_tiles, dilation, rows, w), BF16),
                   jax.ShapeDtypeStruct((b, n_tiles, dilation, rows, LANES), F32)),
        scratch_shapes=[
            pltpu.VMEM((length, w), BF16),
            pltpu.VMEM((length + 2 * DIL_BLOCK, w), BF16),
            pltpu.VMEM((length + 2 * DIL_BLOCK, w), BF16),
            pltpu.VMEM((length, w), BF16),
            pltpu.VMEM((length, LANES), F32),
            pltpu.VMEM((N_HEADS_B, DIL_SUB, DIL_SUB + 2 * DIL_BLOCK), F32),
        ],
        compiler_params=pltpu.CompilerParams(
            dimension_semantics=("arbitrary", "arbitrary"), vmem_limit_bytes=VMEM_LIMIT_BYTES),
        name=f"dil_{dilation}",
    )(bounded, view(q), view(k), view(v))
    return o.reshape(b, s, w), lse.reshape(b, s, LANES)


def _ungroup_rows(val, levels, scratch_refs):
    slabs = val.shape[1] // LANES
    for lvl in reversed(range(levels)):
        block = GROUP_TILE // GROUP_STEP ** lvl
        sub = block // GROUP_STEP
        ref = scratch_refs[lvl]
        for j in range(slabs):
            for lo in range(0, GROUP_TILE, sub):
                blk0, r = (lo // block) * block, (lo % block) // sub
                ref[j, pl.ds(blk0 + r, sub, stride=GROUP_STEP), :] = val[lo:lo + sub, LANES * j:LANES * (j + 1)]
        val = jnp.concatenate([ref[j] for j in range(slabs)], axis=1)
    return val


def _merge_kernel(x_ref, oat_ref, o1_ref, o2_ref, o3_ref, l1_ref, l2_ref, l3_ref,
                  ga_ref, gb_ref, wo_ref, out_ref, *scratch):
    def group_norm(y, g_ref):
        ms = jnp.mean(y * y, axis=-1, keepdims=True)
        return (y * lax.rsqrt(ms + NORM_EPS) * g_ref[...]).astype(BF16)

    oa = oat_ref[0].astype(F32).T

    n_groups = x_ref.shape[1] // GROUP_TILE
    per_group = len(scratch) // n_groups

    def ungroup(ref, levels, first):
        return jnp.concatenate(
            [_ungroup_rows(ref[0, GROUP_TILE * p:GROUP_TILE * (p + 1)].astype(F32), levels,
                           scratch[per_group * p + first:per_group * p + first + levels])
             for p in range(n_groups)], axis=0)

    def limbs(val):
        hi = val.astype(BF16)
        return hi, (val - hi.astype(F32)).astype(BF16)

    l1, l2, l3 = l1_ref[0], ungroup(l2_ref, 1, 3), ungroup(l3_ref, 2, 4)
    mx = jnp.maximum(jnp.maximum(l1, l2), l3)
    e1, e2, e3 = jnp.exp(l1 - mx), jnp.exp(l2 - mx), jnp.exp(l3 - mx)
    inv = 1.0 / (e1 + e2 + e3)
    ri = lax.broadcasted_iota(jnp.int32, (2 * LANES, WIDTH_B), 0) % LANES
    ci = lax.broadcasted_iota(jnp.int32, (2 * LANES, WIDTH_B), 1)
    expand = jnp.where(ri == _lse_lane_of_head(ci // HEAD_DIM), 1.0, 0.0).astype(BF16)

    def widen(wgt):
        return jnp.dot(jnp.concatenate(limbs(wgt), axis=1), expand, preferred_element_type=F32)

    ob = (widen(e1 * inv) * o1_ref[0].astype(F32)
          + widen(e2 * inv) * ungroup(o2_ref, 1, 0)
          + widen(e3 * inv) * ungroup(o3_ref, 2, 1))
    mixed_a = group_norm(oa, ga_ref)
    mixed_b = group_norm(ob, gb_ref)
    out_ref[0] = (x_ref[0]
                  + jnp.dot(mixed_a, wo_ref[:WIDTH_A], preferred_element_type=F32)
                  + jnp.dot(mixed_b, wo_ref[WIDTH_A:], preferred_element_type=F32))


def _merge(x, oat, obs, lses, ga, gb, wo, grouped_dilations):
    b, s, d = x.shape
    tm = MERGE_TM
    assert tuple(grouped_dilations) == (GROUP_STEP, GROUP_STEP ** 2)
    slab = lambda n: pltpu.VMEM((n, GROUP_TILE, LANES), F32)
    wide, narrow = WIDTH_B // LANES, 1
    full = lambda shape: pl.BlockSpec(shape, lambda bi, i: (0,) * len(shape))
    row = lambda w: pl.BlockSpec((1, tm, w), lambda bi, i: (bi, i, 0))
    return pl.pallas_call(
        _merge_kernel,
        grid=(b, s // tm),
        in_specs=[row(d), pl.BlockSpec((1, WIDTH_A, tm), lambda bi, i: (bi, 0, i)),
                  row(WIDTH_B), row(WIDTH_B), row(WIDTH_B), row(LANES), row(LANES), row(LANES),
                  full(ga.shape), full(gb.shape), full(wo.shape)],
        out_specs=row(d),
        out_shape=jax.ShapeDtypeStruct((b, s, d), F32),
        scratch_shapes=[slab(wide), slab(wide), slab(wide), slab(narrow), slab(narrow), slab(narrow)]
        * (tm // GROUP_TILE),
        compiler_params=pltpu.CompilerParams(
            dimension_semantics=("parallel", "parallel"), vmem_limit_bytes=VMEM_LIMIT_BYTES),
        name="merge",
    )(x, oat, *obs, *lses, ga, gb, wo)


def _gelu_tanh(x):
    c = math.sqrt(2.0 / math.pi)
    return 0.5 * x * (1.0 + jnp.tanh(c * (x + 0.044715 * (x * x * x))))


def _ffn_kernel(xp_ref, xc_ref, xn_ref, g2_ref, wup_ref, cw_ref, cb_ref, wdn_ref, out_ref,
                hext_ref, act_ref, *, d_ff):
    tm = xc_ref.shape[1]
    halo = xp_ref.shape[1]
    i = pl.program_id(1)
    rows = tm + 2 * halo

    def normed(x):
        ms = jnp.mean(x * x, axis=-1, keepdims=True)
        return x * lax.rsqrt(ms + NORM_EPS) * g2_ref[...]

    keep_prev = (i > 0).astype(F32)
    keep_next = (i < pl.num_programs(1) - 1).astype(F32)
    hext_ref[0:halo] = (normed(xp_ref[0]) * keep_prev).astype(BF16)
    hext_ref[halo:halo + tm] = normed(xc_ref[0]).astype(BF16)
    hext_ref[halo + tm:] = (normed(xn_ref[0]) * keep_next).astype(BF16)
    hext = hext_ref[...]

    def conv(u, col):
        w = cw_ref[:, col:col + FFN_FC]
        up = pltpu.roll(u, 1, 0)
        dn = pltpu.roll(u, rows - 1, 0)
        y = up * w[0:1] + u * w[1:2] + dn * w[2:3] + cb_ref[:, col:col + FFN_FC]
        return y[halo:halo + tm]

    for c in range(0, d_ff, FFN_FC):
        ug = jnp.dot(hext, wup_ref[:, c:c + FFN_FC], preferred_element_type=F32)
        uv = jnp.dot(hext, wup_ref[:, d_ff + c:d_ff + c + FFN_FC], preferred_element_type=F32)
        act_ref[:, c:c + FFN_FC] = (_gelu_tanh(conv(ug, c)) * conv(uv, d_ff + c)).astype(BF16)

    out_ref[0] = xc_ref[0] + jnp.dot(act_ref[...], wdn_ref[...], preferred_element_type=F32)


def _ffn(x, g2, w_up, conv_w, conv_b, w_down):
    b, s, d = x.shape
    d_ff = w_down.shape[0]
    tm, halo = FFN_TM, FFN_HALO
    per_tile = tm // halo
    n_halo_blocks = s // halo
    full = lambda shape: pl.BlockSpec(shape, lambda bi, i: (0,) * len(shape))
    resident = lambda shape: pl.BlockSpec(shape, lambda bi, i: (0,) * len(shape), pipeline_mode=pl.Buffered(1))
    return pl.pallas_call(
        functools.partial(_ffn_kernel, d_ff=d_ff),
        grid=(b, s // tm),
        in_specs=[
            pl.BlockSpec((1, halo, d), lambda bi, i: (bi, jnp.maximum(i * per_tile - 1, 0), 0)),
            pl.BlockSpec((1, tm, d), lambda bi, i: (bi, i, 0)),
            pl.BlockSpec((1, halo, d), lambda bi, i: (bi, jnp.minimum((i + 1) * per_tile, n_halo_blocks - 1), 0)),
            full(g2.shape), resident(w_up.shape), full(conv_w.shape), full(conv_b.shape), resident(w_down.shape),
        ],
        out_specs=pl.BlockSpec((1, tm, d), lambda bi, i: (bi, i, 0)),
        out_shape=jax.ShapeDtypeStruct((b, s, d), F32),
        scratch_shapes=[pltpu.VMEM((tm + 2 * halo, d), BF16), pltpu.VMEM((tm, d_ff), BF16)],
        compiler_params=pltpu.CompilerParams(
            dimension_semantics=("parallel", "parallel"), vmem_limit_bytes=VMEM_LIMIT_BYTES),
        name="ffn",
    )(x, x, x, g2, w_up, conv_w, conv_b, w_down)


def _rope_tables_t(seq_len):
    rows = seq_len // GRID_W
    row = np.repeat(np.arange(rows, dtype=np.float32), GRID_W)
    col = np.tile(np.arange(GRID_W, dtype=np.float32), rows)
    inv = (np.float32(ROPE_THETA) ** (-np.arange(0, ROPE_AXIS_DIM, 2, dtype=np.float32) / np.float32(ROPE_AXIS_DIM))
           ).astype(np.float32)
    ang_r = (row[:, None] * inv[None, :]).T
    ang_c = (col[:, None] * inv[None, :]).T
    cos_t = np.concatenate([np.cos(ang_r)] * 2 + [np.cos(ang_c)] * 2, axis=0).astype(np.float32)
    sin_t = np.concatenate([-np.sin(ang_r), np.sin(ang_r), -np.sin(ang_c), np.sin(ang_c)], axis=0).astype(np.float32)
    return jnp.asarray(cos_t), jnp.asarray(sin_t)


def _score_bound(gq, gk):
    return math.sqrt(HEAD_DIM) * jnp.max(jnp.abs(gq)) * jnp.max(jnp.abs(gk))


def kernel(x, norm1_g, w_in, qa_norm_g, ka_norm_g, qb_norm_g, kb_norm_g, outa_norm_g, outb_norm_g,
           w_out, norm2_g, w_up, conv_w, conv_b, w_down):
    b, s, d = x.shape
    scale = HEAD_DIM ** -0.5 * math.log2(math.e)
    a_cols = WIDTH_A + 2 * KV_WIDTH_A
    wa = w_in[:, :a_cols].astype(BF16)
    wb = w_in[:, a_cols:].astype(BF16)
    bound_a = _score_bound(qa_norm_g, ka_norm_g)
    bounded_a = (bound_a <= SAFE_SCORE_BOUND).astype(jnp.int32).reshape(1)
    rescale_a = jnp.maximum(bound_a / SAFE_SCORE_BOUND, 1.0)
    positive = lambda g: jnp.where(g > 0, g, 1.0)
    gq_max, gk_max = positive(jnp.max(jnp.abs(qa_norm_g))), positive(jnp.max(jnp.abs(ka_norm_g)))
    ga = jnp.concatenate([jnp.tile(qa_norm_g, N_HEADS_A) * jnp.sqrt(scale * gk_max / (gq_max * rescale_a)),
                          jnp.tile(ka_norm_g, N_KV_A) * jnp.sqrt(scale * gq_max / (gk_max * rescale_a))])
    ga = jnp.broadcast_to(ga[:, None], (ga.shape[0], LANES))
    gqb = (jnp.tile(qb_norm_g, N_HEADS_B) * scale)[None, :]
    gkb = jnp.tile(kb_norm_g, N_HEADS_B)[None, :]
    cos_t, sin_t = _rope_tables_t(s)

    dilations = [dil for _, dil in DILATED_PATTERNS]
    assert dilations[0] == 1
    grouped_dilations = dilations[1:]
    qt, k, vt, *qkv_b = _proj(x, norm1_g[None, :], wa, wb, ga, cos_t, sin_t, gqb, gkb, grouped_dilations)

    oat = _attn_a(bounded_a, rescale_a.reshape(1), qt, k, vt)
    bounded_b = (_score_bound(qb_norm_g, kb_norm_g) <= SAFE_SCORE_BOUND).astype(jnp.int32).reshape(1)
    results = [_dilated_rows(bounded_b, *qkv_b[:3])]
    for n, dil in enumerate(grouped_dilations):
        results.append(_dilated_grouped(bounded_b, *qkv_b[3 * (n + 1):3 * (n + 2)], dil))
    obs, lses = zip(*results)
    x2 = _merge(x, oat, obs, lses, outa_norm_g[None, :], outb_norm_g[None, :], w_out.astype(BF16),
                grouped_dilations)
    return _ffn(x2, norm2_g[None, :], w_up.astype(BF16), conv_w, conv_b[None, :], w_down.astype(BF16))
```

```python
import functools
import math

import jax
import jax.numpy as jnp
import numpy as np
from jax import lax
from jax.experimental import pallas as pl
from jax.experimental.pallas import tpu as pltpu

F32 = jnp.float32
BF16 = jnp.bfloat16
FP8 = jnp.float8_e4m3fn

HEAD_DIM = 64
N_HEADS_A = 8
N_KV_A = 2
GQA_GROUP = N_HEADS_A // N_KV_A
N_HEADS_B = 8
WIDTH_A = N_HEADS_A * HEAD_DIM
WIDTH_B = N_HEADS_B * HEAD_DIM
KV_WIDTH_A = N_KV_A * HEAD_DIM
GRID_W = 64
ROPE_THETA = 10000.0
ROPE_AXIS_DIM = HEAD_DIM // 2
DIL_BLOCK = 64
DILATED_PATTERNS = ((128, 1), (512, 4), (2048, 16))
NORM_EPS = 1e-6
NEG_INF = -1e30

LANES = 128
MXU_DIM = 256
VMEM_LIMIT_BYTES = 56 * 1024 * 1024

PROJ_TM = 1024
GROUP_TILE = 512
GROUP_STEP = 4
ATT_TQ = 2048
ATT_TK = 2048
SAFE_SCORE_BOUND = 40.0
VT_ROWS = HEAD_DIM + 16
DIL_TL = 1024
DIL_SUB = 128
DIL_UNROLL = 4
MERGE_TM = 1024
FFN_TM = 1024
FFN_FC = 256
FFN_HALO = 16
LSE_LANES = 16


def _nt_dot(a, b):
    return lax.dot_general(a, b, (((1,), (1,)), ((), ())), preferred_element_type=F32)


def _proj_kernel(x_ref, g1_ref, wa_ref, wb_ref, ga_ref, cos_ref, sin_ref, gqb_ref, gkb_ref,
                 qt_ref, k_ref, vt_ref, qb_ref, kb_ref, vb_ref, *rest):
    *grouped_refs, nat_all_ref, lvl_all_ref = rest
    for part in range(x_ref.shape[1] // GROUP_TILE):
        _proj_rows(part * GROUP_TILE, x_ref, g1_ref, wa_ref, wb_ref, ga_ref, cos_ref, sin_ref, gqb_ref, gkb_ref,
                   qt_ref, k_ref, vt_ref, qb_ref, kb_ref, vb_ref, grouped_refs,
                   nat_all_ref.at[part], lvl_all_ref.at[part])


def _proj_rows(r0, x_ref, g1_ref, wa_ref, wb_ref, ga_ref, cos_ref, sin_ref, gqb_ref, gkb_ref,
               qt_ref, k_ref, vt_ref, qb_ref, kb_ref, vb_ref, grouped_refs, nat_ref, lvl_ref):
    tm = GROUP_TILE
    rows = slice(r0, r0 + tm)
    x = x_ref[0, rows]
    ms = jnp.mean(x * x, axis=-1, keepdims=True)
    hn = (x * lax.rsqrt(ms + NORM_EPS) * g1_ref[...]).astype(BF16)
    yb = jnp.dot(hn, wb_ref[...], preferred_element_type=F32)

    ya_t = _nt_dot(wa_ref[...], hn)
    ones_rows = jnp.where(lax.broadcasted_iota(jnp.int32, (VT_ROWS - HEAD_DIM, tm), 0) == 0, 1.0, 0.0).astype(BF16)
    for g in range(N_KV_A):
        v_rows = ya_t[WIDTH_A + KV_WIDTH_A + HEAD_DIM * g:WIDTH_A + KV_WIDTH_A + HEAD_DIM * (g + 1)]
        vt_ref[0, g, :, rows] = jnp.concatenate([v_rows.astype(BF16), ones_rows], axis=0)
    cos_t = cos_ref[:, rows]
    sin_t = sin_ref[:, rows]
    reps = tm // LANES
    zeros = jnp.zeros((HEAD_DIM, tm), F32)
    q16 = ROPE_AXIS_DIM // 2

    def limbs8(val):
        hi = val.astype(FP8).astype(F32)
        return hi, (val - hi).astype(FP8).astype(F32)

    for h in range(N_HEADS_A + N_KV_A):
        u = ya_t[HEAD_DIM * h:HEAD_DIM * (h + 1)]
        r = lax.rsqrt(jnp.mean(u * u, axis=0, keepdims=True) + NORM_EPS)
        g = jnp.tile(ga_ref[HEAD_DIM * h:HEAD_DIM * (h + 1), :], (1, reps))
        un = u * r * g
        partner = jnp.concatenate(
            [un[q16:2 * q16], un[0:q16], un[3 * q16:4 * q16], un[2 * q16:3 * q16]], axis=0)
        hi, lo = limbs8(un * cos_t + partner * sin_t)
        if h < N_HEADS_A:
            qt_ref[0, h, :, rows] = jnp.concatenate([hi, lo, hi, zeros], axis=0).astype(FP8)
        else:
            k_ref[0, h - N_HEADS_A, rows] = jnp.concatenate([hi, hi, lo, zeros], axis=0).T.astype(FP8)

    ri = lax.broadcasted_iota(jnp.int32, (MXU_DIM, MXU_DIM), 0) // HEAD_DIM
    ci = lax.broadcasted_iota(jnp.int32, (MXU_DIM, MXU_DIM), 1) // HEAD_DIM
    ones_bd = jnp.where(ri == ci, 1.0, 0.0).astype(BF16)

    def head_norm(y, g_ref):
        sq = (y * y).astype(BF16)
        ss = jnp.concatenate(
            [jnp.dot(sq[:, c:c + MXU_DIM], ones_bd, preferred_element_type=F32)
             for c in range(0, y.shape[1], MXU_DIM)], axis=1)
        return y * lax.rsqrt(ss * (1.0 / HEAD_DIM) + NORM_EPS) * g_ref[...]

    parts = (head_norm(yb[:, :WIDTH_B], gqb_ref), head_norm(yb[:, WIDTH_B:2 * WIDTH_B], gkb_ref),
             yb[:, 2 * WIDTH_B:])
    slabs = WIDTH_B // LANES
    for c, (part, ref) in enumerate(zip(parts, (qb_ref, kb_ref, vb_ref))):
        ref[0, rows] = part.astype(BF16)
        for jj in range(slabs):
            nat_ref[slabs * c + jj] = part[:, LANES * jj:LANES * (jj + 1)]

    src_ref, block = nat_ref, tm
    n_levels = len(grouped_refs) // 3
    for n in range(n_levels):
        sub = block // GROUP_STEP
        for j in range(3 * slabs):
            c, jj = divmod(j, slabs)
            dst_ref = grouped_refs[3 * n + c]
            for lo in range(0, tm, sub):
                blk0, r = (lo // block) * block, (lo % block) // sub
                picked = src_ref[j, pl.ds(blk0 + r, sub, stride=GROUP_STEP), :]
                if n + 1 < n_levels:
                    lvl_ref[j, lo:lo + sub, :] = picked
                dst_ref[0, r0 + lo:r0 + lo + sub, LANES * jj:LANES * (jj + 1)] = picked.astype(BF16)
        src_ref, block = lvl_ref, sub


def _group_index(tau, tm, dilation):
    base, size, local = 0, tm, tau
    while dilation > 1:
        size //= GROUP_STEP
        base = base + (local % GROUP_STEP) * size
        local = local // GROUP_STEP
        dilation //= GROUP_STEP
    return base + local


def _proj(x, g1, wa, wb, ga, cos_t, sin_t, gqb, gkb, grouped_dilations):
    b, s, d = x.shape
    tm = PROJ_TM
    assert tuple(grouped_dilations) == tuple(GROUP_STEP ** (n + 1) for n in range(len(grouped_dilations)))
    assert len(grouped_dilations) <= 2
    full = lambda shape: pl.BlockSpec(shape, lambda bi, i: (0,) * len(shape))
    rows_b = jax.ShapeDtypeStruct((b, s, WIDTH_B), BF16)
    out_shape = (
        jax.ShapeDtypeStruct((b, N_HEADS_A, MXU_DIM, s), FP8),
        jax.ShapeDtypeStruct((b, N_KV_A, s, MXU_DIM), FP8),
        jax.ShapeDtypeStruct((b, N_KV_A, VT_ROWS, s), BF16),
    ) + (rows_b,) * (3 + 3 * len(grouped_dilations))
    row = lambda w: pl.BlockSpec((1, tm, w), lambda bi, i: (bi, i, 0))
    return pl.pallas_call(
        _proj_kernel,
        grid=(b, s // tm),
        in_specs=[
            row(d), full(g1.shape), full(wa.shape), full(wb.shape), full(ga.shape),
            pl.BlockSpec((HEAD_DIM, tm), lambda bi, i: (0, i)),
            pl.BlockSpec((HEAD_DIM, tm), lambda bi, i: (0, i)),
            full(gqb.shape), full(gkb.shape),
        ],
        out_specs=(
            pl.BlockSpec((1, N_HEADS_A, MXU_DIM, tm), lambda bi, i: (bi, 0, 0, i)),
            pl.BlockSpec((1, N_KV_A, tm, MXU_DIM), lambda bi, i: (bi, 0, i, 0)),
            pl.BlockSpec((1, N_KV_A, VT_ROWS, tm), lambda bi, i: (bi, 0, 0, i)),
        ) + (row(WIDTH_B),) * (3 + 3 * len(grouped_dilations)),
        out_shape=out_shape,
        scratch_shapes=[pltpu.VMEM((tm // GROUP_TILE, 3 * WIDTH_B // LANES, GROUP_TILE, LANES), F32)] * 2,
        compiler_params=pltpu.CompilerParams(
            dimension_semantics=("parallel", "parallel"), vmem_limit_bytes=VMEM_LIMIT_BYTES),
        name="proj",
    )(x, g1, wa, wb, ga, cos_t, sin_t, gqb, gkb)


def _attn_a_kernel(bounded_ref, rescale_ref, qt_ref, k_ref, vt_ref, ot_ref, m_ref, acc_ref):
    j = pl.program_id(3)
    bounded = bounded_ref[0] != 0

    @pl.when(j == 0)
    def _():
        m_ref[...] = jnp.full(m_ref.shape, NEG_INF, F32)
        acc_ref[...] = jnp.zeros(acc_ref.shape, F32)

    k = k_ref[0, 0]
    vt = vt_ref[0, 0]

    def scores(h):
        return jnp.dot(k, qt_ref[0, h], preferred_element_type=F32) * rescale_ref[0]

    @pl.when(bounded)
    def _():
        tk, tq = k_ref.shape[2], qt_ref.shape[3]
        for h in range(GQA_GROUP):
            for c in range(0, tq, MXU_DIM):
                q_c = qt_ref[0, h, :, c:c + MXU_DIM]
                part = None
                for r in range(0, tk, MXU_DIM):
                    s_t = jnp.dot(k_ref[0, 0, r:r + MXU_DIM, :], q_c, preferred_element_type=F32)
                    pv = jnp.dot(vt_ref[0, 0, :, r:r + MXU_DIM], jnp.exp2(s_t).astype(BF16),
                                 preferred_element_type=F32)
                    part = pv if part is None else part + pv
                acc_ref[h, :, c:c + MXU_DIM] += part

    @pl.when(jnp.logical_not(bounded))
    def _():
        for h in range(GQA_GROUP):
            s_t = scores(h)
            m_prev = m_ref[h]
            m_new = jnp.maximum(m_prev, jnp.max(s_t, axis=0, keepdims=True))
            p_t = jnp.exp2(s_t - m_new).astype(BF16)
            acc_ref[h] = (jnp.exp2(m_prev - m_new) * acc_ref[h]
                          + jnp.dot(vt, p_t, preferred_element_type=F32))
            m_ref[h] = m_new

    @pl.when(j == pl.num_programs(3) - 1)
    def _():
        for h in range(GQA_GROUP):
            a = acc_ref[h]
            ot_ref[0, HEAD_DIM * h:HEAD_DIM * (h + 1), :] = (
                a[:HEAD_DIM] * (1.0 / a[HEAD_DIM:HEAD_DIM + 1])).astype(BF16)


def _attn_a(bounded, rescale, qt, k, vt):
    b, _, _, s = qt.shape
    tq, tk = ATT_TQ, ATT_TK
    gw = GQA_GROUP * HEAD_DIM
    return pl.pallas_call(
        _attn_a_kernel,
        grid=(b, N_KV_A, s // tq, s // tk),
        in_specs=[
            pl.BlockSpec(memory_space=pltpu.SMEM),
            pl.BlockSpec(memory_space=pltpu.SMEM),
            pl.BlockSpec((1, GQA_GROUP, MXU_DIM, tq), lambda bi, g, i, j: (bi, g, 0, i)),
            pl.BlockSpec((1, 1, tk, MXU_DIM), lambda bi, g, i, j: (bi, g, j, 0)),
            pl.BlockSpec((1, 1, VT_ROWS, tk), lambda bi, g, i, j: (bi, g, 0, j)),
        ],
        out_specs=pl.BlockSpec((1, gw, tq), lambda bi, g, i, j: (bi, g, i)),
        out_shape=jax.ShapeDtypeStruct((b, WIDTH_A, s), BF16),
        scratch_shapes=[
            pltpu.VMEM((GQA_GROUP, 1, tq), F32),
            pltpu.VMEM((GQA_GROUP, VT_ROWS, tq), F32),
        ],
        compiler_params=pltpu.CompilerParams(
            dimension_semantics=("parallel", "parallel", "parallel", "arbitrary"),
            vmem_limit_bytes=VMEM_LIMIT_BYTES),
        name="attn_a",
    )(bounded, rescale, qt, k, vt)


def _dil_bias_init(bias_ref, dilation):
    tk = DIL_SUB + 2 * DIL_BLOCK
    qi = lax.broadcasted_iota(jnp.int32, (DIL_SUB, tk), 0)
    kj = lax.broadcasted_iota(jnp.int32, (DIL_SUB, tk), 1)
    aoff = jnp.abs(kj - DIL_BLOCK - qi)
    dist = (aoff * dilation).astype(F32)
    for h in range(N_HEADS_B):
        slope = 2.0 ** (-8.0 * (h + 1) / N_HEADS_B) * math.log2(math.e)
        bias_ref[h] = jnp.where(aoff <= DIL_BLOCK, -slope * dist, NEG_INF)


def _lse_lane_pair(lane):
    return (lane % HEAD_DIM) // LSE_LANES


def _lse_lane_of_head(h):
    return HEAD_DIM * (h % 2) + LSE_LANES * (h // 2)


def _dil_sub_tiles(bounded, q_ref, kext_ref, vext_ref, o_ref, lse_ref, bias_ref, *, n_sub, first_key, length):
    tk = DIL_SUB + 2 * DIL_BLOCK
    first_head = lax.broadcasted_iota(jnp.int32, (DIL_SUB, LANES), 1) < HEAD_DIM
    first_head_k = lax.broadcasted_iota(jnp.int32, (tk, LANES), 1) < HEAD_DIM
    lse_owner = _lse_lane_pair(lax.broadcasted_iota(jnp.int32, (DIL_SUB, LANES), 1))
    key_lane = lax.broadcasted_iota(jnp.int32, (1, tk), 1)
    stack_row = lax.broadcasted_iota(jnp.int32, (2 * tk, LANES), 0) < tk
    stack_lane = lax.broadcasted_iota(jnp.int32, (2 * tk, LANES), 1) < HEAD_DIM
    ones_stack = jnp.where(stack_row == stack_lane, 1.0, 0.0).astype(BF16)

    def sub_tile(st, carry, *, use_max):
        r0 = pl.multiple_of(st * DIL_SUB, DIL_SUB)
        kidx = first_key + r0 + key_lane
        in_range = (kidx >= 0) & (kidx < length)
        lse_tile = jnp.zeros((DIL_SUB, LANES), F32)
        for hp in range(N_HEADS_B // 2):
            cols = slice(LANES * hp, LANES * (hp + 1))
            qp = q_ref[pl.ds(r0, DIL_SUB), cols]
            kp = kext_ref[pl.ds(r0, tk), cols]
            vp = vext_ref[pl.ds(r0, tk), cols]
            zeros_v = jnp.zeros_like(vp)
            v_stack = jnp.concatenate([jnp.where(first_head_k, vp, zeros_v),
                                       jnp.where(first_head_k, zeros_v, vp)], axis=0)
            ps, ms = [], []
            for e in range(2):
                own = first_head if e == 0 else jnp.logical_not(first_head)
                qm = jnp.where(own, qp, jnp.zeros_like(qp))
                s = _nt_dot(qm, kp) + bias_ref[2 * hp + e]
                s = jnp.where(in_range, s, NEG_INF)
                if use_max:
                    ms.append(jnp.max(s, axis=-1, keepdims=True))
                    s = s - ms[-1]
                ps.append(jnp.exp2(s).astype(BF16))
            p_pair = jnp.concatenate(ps, axis=1)
            both = jnp.dot(p_pair, jnp.concatenate([v_stack, ones_stack], axis=1), preferred_element_type=F32)
            pv, l_pair = both[:, :LANES], both[:, LANES:]
            o_ref[pl.ds(r0, DIL_SUB), cols] = (pv * (1.0 / l_pair)).astype(BF16)
            lse = jnp.log(l_pair)
            if use_max:
                lse = lse + jnp.where(first_head, ms[0], ms[1]) * math.log(2.0)
            lse_tile = jnp.where(lse_owner == hp, lse, lse_tile)
        lse_ref[pl.ds(r0, DIL_SUB), :] = lse_tile
        return carry

    @pl.when(bounded)
    def _():
        lax.fori_loop(0, n_sub, functools.partial(sub_tile, use_max=False), 0, unroll=DIL_UNROLL)

    @pl.when(jnp.logical_not(bounded))
    def _():
        lax.fori_loop(0, n_sub, functools.partial(sub_tile, use_max=True), 0)


def _dil_rows_kernel(bounded_ref, q_ref, kp_ref, kc_ref, kn_ref, vp_ref, vc_ref, vn_ref, o_ref, lse_ref,
                     kext_ref, vext_ref, bias_ref, *, length):
    tl = q_ref.shape[1]
    bi, i = pl.program_id(0), pl.program_id(1)

    @pl.when((bi == 0) & (i == 0))
    def _():
        _dil_bias_init(bias_ref, 1)

    kext_ref[0:DIL_BLOCK] = kp_ref[0]
    kext_ref[DIL_BLOCK:DIL_BLOCK + tl] = kc_ref[0]
    kext_ref[DIL_BLOCK + tl:] = kn_ref[0]
    vext_ref[0:DIL_BLOCK] = vp_ref[0]
    vext_ref[DIL_BLOCK:DIL_BLOCK + tl] = vc_ref[0]
    vext_ref[DIL_BLOCK + tl:] = vn_ref[0]
    _dil_sub_tiles(bounded_ref[0] != 0, q_ref.at[0], kext_ref, vext_ref, o_ref.at[0], lse_ref.at[0], bias_ref,
                   n_sub=tl // DIL_SUB, first_key=i * tl - DIL_BLOCK, length=length)


def _dilated_rows(bounded, q, k, v):
    b, s, w = q.shape
    tl = DIL_TL
    blocks_per_tile = tl // DIL_BLOCK
    n_blocks = s // DIL_BLOCK
    cur = pl.BlockSpec((1, tl, w), lambda bi, i: (bi, i, 0))
    prev = pl.BlockSpec((1, DIL_BLOCK, w), lambda bi, i: (bi, jnp.maximum(i * blocks_per_tile - 1, 0), 0))
    nxt = pl.BlockSpec((1, DIL_BLOCK, w),
                       lambda bi, i: (bi, jnp.minimum((i + 1) * blocks_per_tile, n_blocks - 1), 0))
    return pl.pallas_call(
        functools.partial(_dil_rows_kernel, length=s),
        grid=(b, s // tl),
        in_specs=[pl.BlockSpec(memory_space=pltpu.SMEM), cur, prev, cur, nxt, prev, cur, nxt],
        out_specs=(cur, pl.BlockSpec((1, tl, LANES), lambda bi, i: (bi, i, 0))),
        out_shape=(jax.ShapeDtypeStruct((b, s, w), BF16), jax.ShapeDtypeStruct((b, s, LANES), F32)),
        scratch_shapes=[
            pltpu.VMEM((tl + 2 * DIL_BLOCK, w), BF16),
            pltpu.VMEM((tl + 2 * DIL_BLOCK, w), BF16),
            pltpu.VMEM((N_HEADS_B, DIL_SUB, DIL_SUB + 2 * DIL_BLOCK), F32),
        ],
        compiler_params=pltpu.CompilerParams(
            dimension_semantics=("arbitrary", "arbitrary"), vmem_limit_bytes=VMEM_LIMIT_BYTES),
        name="dil_1",
    )(bounded, q, k, k, k, v, v, v)


def _dil_grouped_kernel(bounded_ref, q_ref, k_ref, v_ref, o_ref, lse_ref,
                        qflat_ref, kext_ref, vext_ref, oflat_ref, lseflat_ref, bias_ref, *, dilation):
    n_tiles, rows = q_ref.shape[1], q_ref.shape[2]
    length = n_tiles * rows
    bi, r = pl.program_id(0), pl.program_id(1)

    @pl.when((bi == 0) & (r == 0))
    def _():
        _dil_bias_init(bias_ref, dilation)

    border = jnp.zeros((DIL_BLOCK, kext_ref.shape[1]), BF16)
    for ext_ref in (kext_ref, vext_ref):
        ext_ref[0:DIL_BLOCK] = border
        ext_ref[DIL_BLOCK + length:] = border
    for t in range(n_tiles):
        qflat_ref[rows * t:rows * (t + 1)] = q_ref[0, t]
        kext_ref[DIL_BLOCK + rows * t:DIL_BLOCK + rows * (t + 1)] = k_ref[0, t]
        vext_ref[DIL_BLOCK + rows * t:DIL_BLOCK + rows * (t + 1)] = v_ref[0, t]
    _dil_sub_tiles(bounded_ref[0] != 0, qflat_ref, kext_ref, vext_ref, oflat_ref, lseflat_ref, bias_ref,
                   n_sub=length // DIL_SUB, first_key=-DIL_BLOCK, length=length)
    for t in range(n_tiles):
        o_ref[0, t] = oflat_ref[rows * t:rows * (t + 1)]
        lse_ref[0, t] = lseflat_ref[rows * t:rows * (t + 1)]


def _dilated_grouped(bounded, q, k, v, dilation):
    b, s, w = q.shape
    n_tiles, rows = s // GROUP_TILE, GROUP_TILE // dilation
    length = n_tiles * rows
    view = lambda a: a.reshape(b, n_tiles, dilation, rows, a.shape[-1])
    spec = lambda width: pl.BlockSpec((1, n_tiles, None, rows, width), lambda bi, r: (bi, 0, r, 0, 0))
    o, lse = pl.pallas_call(
        functools.partial(_dil_grouped_kernel, dilation=dilation),
        grid=(b, dilation),
        in_specs=[pl.BlockSpec(memory_space=pltpu.SMEM), spec(w), spec(w), spec(w)],
        out_specs=(spec(w), spec(LANES)),
        out_shape=(jax.ShapeDtypeStruct((b, n_tiles, dilation, rows, w), BF16),
                   jax.ShapeDtypeStruct((b, n_tiles, dilation, rows, LANES), F32)),
        scratch_shapes=[
            pltpu.VMEM((length, w), BF16),
            pltpu.VMEM((length + 2 * DIL_BLOCK, w), BF16),
            pltpu.VMEM((length + 2 * DIL_BLOCK, w), BF16),
            pltpu.VMEM((length, w), BF16),
            pltpu.VMEM((length, LANES), F32),
            pltpu.VMEM((N_HEADS_B, DIL_SUB, DIL_SUB + 2 * DIL_BLOCK), F32),
        ],
        compiler_params=pltpu.CompilerParams(
            dimension_semantics=("arbitrary", "arbitrary"), vmem_limit_bytes=VMEM_LIMIT_BYTES),
        name=f"dil_{dilation}",
    )(bounded, view(q), view(k), view(v))
    return o.reshape(b, s, w), lse.reshape(b, s, LANES)


def _ungroup_rows(val, levels, scratch_refs):
    slabs = val.shape[1] // LANES
    for lvl in reversed(range(levels)):
        block = GROUP_TILE // GROUP_STEP ** lvl
        sub = block // GROUP_STEP
        ref = scratch_refs[lvl]
        for j in range(slabs):
            for lo in range(0, GROUP_TILE, sub):
                blk0, r = (lo // block) * block, (lo % block) // sub
                ref[j, pl.ds(blk0 + r, sub, stride=GROUP_STEP), :] = val[lo:lo + sub, LANES * j:LANES * (j + 1)]
        val = jnp.concatenate([ref[j] for j in range(slabs)], axis=1)
    return val


def _merge_kernel(x_ref, oat_ref, o1_ref, o2_ref, o3_ref, l1_ref, l2_ref, l3_ref,
                  ga_ref, gb_ref, wo_ref, out_ref, *scratch):
    def group_norm(y, g_ref):
        ms = jnp.mean(y * y, axis=-1, keepdims=True)
        return (y * lax.rsqrt(ms + NORM_EPS) * g_ref[...]).astype(BF16)

    oa = oat_ref[0].astype(F32).T

    n_groups = x_ref.shape[1] // GROUP_TILE
    per_group = len(scratch) // n_groups

    def ungroup(ref, levels, first):
        return jnp.concatenate(
            [_ungroup_rows(ref[0, GROUP_TILE * p:GROUP_TILE * (p + 1)].astype(F32), levels,
                           scratch[per_group * p + first:per_group * p + first + levels])
             for p in range(n_groups)], axis=0)

    def limbs(val):
        hi = val.astype(BF16)
        return hi, (val - hi.astype(F32)).astype(BF16)

    l1, l2, l3 = l1_ref[0], ungroup(l2_ref, 1, 3), ungroup(l3_ref, 2, 4)
    mx = jnp.maximum(jnp.maximum(l1, l2), l3)
    e1, e2, e3 = jnp.exp(l1 - mx), jnp.exp(l2 - mx), jnp.exp(l3 - mx)
    inv = 1.0 / (e1 + e2 + e3)
    ri = lax.broadcasted_iota(jnp.int32, (2 * LANES, WIDTH_B), 0) % LANES
    ci = lax.broadcasted_iota(jnp.int32, (2 * LANES, WIDTH_B), 1)
    expand = jnp.where(ri == _lse_lane_of_head(ci // HEAD_DIM), 1.0, 0.0).astype(BF16)

    def widen(wgt):
        return jnp.dot(jnp.concatenate(limbs(wgt), axis=1), expand, preferred_element_type=F32)

    ob = (widen(e1 * inv) * o1_ref[0].astype(F32)
          + widen(e2 * inv) * ungroup(o2_ref, 1, 0)
          + widen(e3 * inv) * ungroup(o3_ref, 2, 1))
    mixed_a = group_norm(oa, ga_ref)
    mixed_b = group_norm(ob, gb_ref)
    out_ref[0] = (x_ref[0]
                  + jnp.dot(mixed_a, wo_ref[:WIDTH_A], preferred_element_type=F32)
                  + jnp.dot(mixed_b, wo_ref[WIDTH_A:], preferred_element_type=F32))


def _merge(x, oat, obs, lses, ga, gb, wo, grouped_dilations):
    b, s, d = x.shape
    tm = MERGE_TM
    assert tuple(grouped_dilations) == (GROUP_STEP, GROUP_STEP ** 2)
    slab = lambda n: pltpu.VMEM((n, GROUP_TILE, LANES), F32)
    wide, narrow = WIDTH_B // LANES, 1
    full = lambda shape: pl.BlockSpec(shape, lambda bi, i: (0,) * len(shape))
    row = lambda w: pl.BlockSpec((1, tm, w), lambda bi, i: (bi, i, 0))
    return pl.pallas_call(
        _merge_kernel,
        grid=(b, s // tm),
        in_specs=[row(d), pl.BlockSpec((1, WIDTH_A, tm), lambda bi, i: (bi, 0, i)),
                  row(WIDTH_B), row(WIDTH_B), row(WIDTH_B), row(LANES), row(LANES), row(LANES),
                  full(ga.shape), full(gb.shape), full(wo.shape)],
        out_specs=row(d),
        out_shape=jax.ShapeDtypeStruct((b, s, d), F32),
        scratch_shapes=[slab(wide), slab(wide), slab(wide), slab(narrow), slab(narrow), slab(narrow)]
        * (tm // GROUP_TILE),
        compiler_params=pltpu.CompilerParams(
            dimension_semantics=("parallel", "parallel"), vmem_limit_bytes=VMEM_LIMIT_BYTES),
        name="merge",
    )(x, oat, *obs, *lses, ga, gb, wo)


def _gelu_tanh(x):
    c = math.sqrt(2.0 / math.pi)
    return 0.5 * x * (1.0 + jnp.tanh(c * (x + 0.044715 * (x * x * x))))


def _ffn_kernel(xp_ref, xc_ref, xn_ref, g2_ref, wup_ref, cw_ref, cb_ref, wdn_ref, out_ref,
                hext_ref, act_ref, *, d_ff):
    tm = xc_ref.shape[1]
    halo = xp_ref.shape[1]
    i = pl.program_id(1)
    rows = tm + 2 * halo

    def normed(x):
        ms = jnp.mean(x * x, axis=-1, keepdims=True)
        return x * lax.rsqrt(ms + NORM_EPS) * g2_ref[...]

    keep_prev = (i > 0).astype(F32)
    keep_next = (i < pl.num_programs(1) - 1).astype(F32)
    hext_ref[0:halo] = (normed(xp_ref[0]) * keep_prev).astype(BF16)
    hext_ref[halo:halo + tm] = normed(xc_ref[0]).astype(BF16)
    hext_ref[halo + tm:] = (normed(xn_ref[0]) * keep_next).astype(BF16)
    hext = hext_ref[...]

    def conv(u, col):
        w = cw_ref[:, col:col + FFN_FC]
        up = pltpu.roll(u, 1, 0)
        dn = pltpu.roll(u, rows - 1, 0)
        y = up * w[0:1] + u * w[1:2] + dn * w[2:3] + cb_ref[:, col:col + FFN_FC]
        return y[halo:halo + tm]

    for c in range(0, d_ff, FFN_FC):
        ug = jnp.dot(hext, wup_ref[:, c:c + FFN_FC], preferred_element_type=F32)
        uv = jnp.dot(hext, wup_ref[:, d_ff + c:d_ff + c + FFN_FC], preferred_element_type=F32)
        act_ref[:, c:c + FFN_FC] = (_gelu_tanh(conv(ug, c)) * conv(uv, d_ff + c)).astype(BF16)

    out_ref[0] = xc_ref[0] + jnp.dot(act_ref[...], wdn_ref[...], preferred_element_type=F32)


def _ffn(x, g2, w_up, conv_w, conv_b, w_down):
    b, s, d = x.shape
    d_ff = w_down.shape[0]
    tm, halo = FFN_TM, FFN_HALO
    per_tile = tm // halo
    n_halo_blocks = s // halo
    full = lambda shape: pl.BlockSpec(shape, lambda bi, i: (0,) * len(shape))
    resident = lambda shape: pl.BlockSpec(shape, lambda bi, i: (0,) * len(shape), pipeline_mode=pl.Buffered(1))
    return pl.pallas_call(
        functools.partial(_ffn_kernel, d_ff=d_ff),
        grid=(b, s // tm),
        in_specs=[
            pl.BlockSpec((1, halo, d), lambda bi, i: (bi, jnp.maximum(i * per_tile - 1, 0), 0)),
            pl.BlockSpec((1, tm, d), lambda bi, i: (bi, i, 0)),
            pl.BlockSpec((1, halo, d), lambda bi, i: (bi, jnp.minimum((i + 1) * per_tile, n_halo_blocks - 1), 0)),
            full(g2.shape), resident(w_up.shape), full(conv_w.shape), full(conv_b.shape), resident(w_down.shape),
        ],
        out_specs=pl.BlockSpec((1, tm, d), lambda bi, i: (bi, i, 0)),
        out_shape=jax.ShapeDtypeStruct((b, s, d), F32),
        scratch_shapes=[pltpu.VMEM((tm + 2 * halo, d), BF16), pltpu.VMEM((tm, d_ff), BF16)],
        compiler_params=pltpu.CompilerParams(
            dimension_semantics=("parallel", "parallel"), vmem_limit_bytes=VMEM_LIMIT_BYTES),
        name="ffn",
    )(x, x, x, g2, w_up, conv_w, conv_b, w_down)


def _rope_tables_t(seq_len):
    rows = seq_len // GRID_W
    row = np.repeat(np.arange(rows, dtype=np.float32), GRID_W)
    col = np.tile(np.arange(GRID_W, dtype=np.float32), rows)
    inv = (np.float32(ROPE_THETA) ** (-np.arange(0, ROPE_AXIS_DIM, 2, dtype=np.float32) / np.float32(ROPE_AXIS_DIM))
           ).astype(np.float32)
    ang_r = (row[:, None] * inv[None, :]).T
    ang_c = (col[:, None] * inv[None, :]).T
    cos_t = np.concatenate([np.cos(ang_r)] * 2 + [np.cos(ang_c)] * 2, axis=0).astype(np.float32)
    sin_t = np.concatenate([-np.sin(ang_r), np.sin(ang_r), -np.sin(ang_c), np.sin(ang_c)], axis=0).astype(np.float32)
    return jnp.asarray(cos_t), jnp.asarray(sin_t)


def _score_bound(gq, gk):
    return math.sqrt(HEAD_DIM) * jnp.max(jnp.abs(gq)) * jnp.max(jnp.abs(gk))


def kernel(x, norm1_g, w_in, qa_norm_g, ka_norm_g, qb_norm_g, kb_norm_g, outa_norm_g, outb_norm_g,
           w_out, norm2_g, w_up, conv_w, conv_b, w_down):
    b, s, d = x.shape
    scale = HEAD_DIM ** -0.5 * math.log2(math.e)
    a_cols = WIDTH_A + 2 * KV_WIDTH_A
    wa = w_in[:, :a_cols].T.astype(BF16)
    wb = w_in[:, a_cols:].astype(BF16)
    bound_a = _score_bound(qa_norm_g, ka_norm_g)
    bounded_a = (bound_a <= SAFE_SCORE_BOUND).astype(jnp.int32).reshape(1)
    rescale_a = jnp.maximum(bound_a / SAFE_SCORE_BOUND, 1.0)
    positive = lambda g: jnp.where(g > 0, g, 1.0)
    gq_max, gk_max = positive(jnp.max(jnp.abs(qa_norm_g))), positive(jnp.max(jnp.abs(ka_norm_g)))
    ga = jnp.concatenate([jnp.tile(qa_norm_g, N_HEADS_A) * jnp.sqrt(scale * gk_max / (gq_max * rescale_a)),
                          jnp.tile(ka_norm_g, N_KV_A) * jnp.sqrt(scale * gq_max / (gk_max * rescale_a))])
    ga = jnp.broadcast_to(ga[:, None], (ga.shape[0], LANES))
    gqb = (jnp.tile(qb_norm_g, N_HEADS_B) * scale)[None, :]
    gkb = jnp.tile(kb_norm_g, N_HEADS_B)[None, :]
    cos_t, sin_t = _rope_tables_t(s)

    dilations = [dil for _, dil in DILATED_PATTERNS]
    assert dilations[0] == 1
    grouped_dilations = dilations[1:]
    qt, k, vt, *qkv_b = _proj(x, norm1_g[None, :], wa, wb, ga, cos_t, sin_t, gqb, gkb, grouped_dilations)

    oat = _attn_a(bounded_a, rescale_a.reshape(1), qt, k, vt)
    bounded_b = (_score_bound(qb_norm_g, kb_norm_g) <= SAFE_SCORE_BOUND).astype(jnp.int32).reshape(1)
    results = [_dilated_rows(bounded_b, *qkv_b[:3])]
    for n, dil in enumerate(grouped_dilations):
        results.append(_dilated_grouped(bounded_b, *qkv_b[3 * (n + 1):3 * (n + 2)], dil))
    obs, lses = zip(*results)
    x2 = _merge(x, oat, obs, lses, outa_norm_g[None, :], outb_norm_g[None, :], w_out.astype(BF16),
                grouped_dilations)
    return _ffn(x2, norm2_g[None, :], w_up.astype(BF16), conv_w, conv_b[None, :], w_down.astype(BF16))
```
